```python
import jax, jax.numpy as jnp
from jax import lax
import numpy as np

D_MODEL = 1024
BATCH = 16
SEQ = 2048
DEPTH = 2
DEC_BATCH = 8
DEC_SEQ = 64
PAST_LEN = 2048

CHUNK = 64
EPS = 1e-6
N_BRANCH = 3
S5_WIDTH = D_MODEL // 2
S5_GROUP = 16
S5_GROUPS = S5_WIDTH // S5_GROUP
S5_STATE = 64
SSD_INNER = D_MODEL
SSD_HEAD_DIM = 64
SSD_HEADS = SSD_INNER // SSD_HEAD_DIM
SSD_GROUPS = 4
SSD_HPG = SSD_HEADS // SSD_GROUPS
SSD_STATE = 128
SSD_CONV = 4
SSD_CONV_DIM = SSD_INNER + 2 * SSD_GROUPS * SSD_STATE
HG_WIDTH = D_MODEL // 2
HG_EXPAND = 128
HG_HEADS = HG_WIDTH // HG_EXPAND
FFN_DIM = 2816
FFN_CONV = 3

IN_SIZES = (S5_WIDTH, SSD_INNER, SSD_CONV_DIM, SSD_HEADS, HG_WIDTH, HG_WIDTH, HG_WIDTH, HG_WIDTH, N_BRANCH * D_MODEL)
IN_TOTAL = sum(IN_SIZES)
IN_SPLITS = tuple(int(v) for v in np.cumsum(IN_SIZES)[:-1])

kernel_name = "hybrid_s5_ssd_hgrn2_stream_step"

F32 = jnp.float32


def rmsnorm(x, w):
    xf = x.astype(F32)
    y = xf * lax.rsqrt(jnp.mean(xf * xf, axis=-1, keepdims=True) + EPS)
    return (y * w.astype(F32)).astype(x.dtype)


def causal_dwconv(x, buf, w, b):
    k = w.shape[0]
    n = x.shape[1]
    full = jnp.concatenate([buf.astype(x.dtype), x], axis=1)
    y = b + sum(full[:, j:j + n] * w[j] for j in range(k))
    return y, full[:, n:]


def to_chunks(a, t):
    b, n = a.shape[:2]
    return jnp.swapaxes(a.reshape((b, n // t, t) + a.shape[2:]), 0, 1)


def from_chunks(a):
    nc, b, t = a.shape[:3]
    return jnp.swapaxes(a, 0, 1).reshape((b, nc * t) + a.shape[3:])


def s5_mixer(u, h_re, h_im, lam_re, lam_im, log_dt, b_re, b_im, c_re, c_im, d_skip, glu_w, glu_b):
    bt, n, _ = u.shape
    uf = u.astype(F32)
    lam_re, lam_im = lam_re.astype(F32), lam_im.astype(F32)
    b_re, b_im, c_re, c_im = b_re.astype(F32), b_im.astype(F32), c_re.astype(F32), c_im.astype(F32)
    dt = jnp.exp(log_dt.astype(F32))[:, None]
    mag = jnp.exp(lam_re * dt)
    ar, ai = mag * jnp.cos(lam_im * dt), mag * jnp.sin(lam_im * dt)
    den = lam_re * lam_re + lam_im * lam_im
    nr, ni = ar - 1.0, ai
    cr = (nr * lam_re + ni * lam_im) / den
    ci = (ni * lam_re - nr * lam_im) / den
    bbr = cr[..., None] * b_re - ci[..., None] * b_im
    bbi = cr[..., None] * b_im + ci[..., None] * b_re
    t = min(CHUNK, n)
    ug = to_chunks(uf.reshape(bt, n, S5_GROUPS, S5_GROUP), t)

    def combine(e1, e2):
        a1r, a1i, b1r, b1i = e1
        a2r, a2i, b2r, b2i = e2
        return (a2r * a1r - a2i * a1i, a2r * a1i + a2i * a1r,
                a2r * b1r - a2i * b1i + b2r, a2r * b1i + a2i * b1r + b2i)

    def step(carry, uc):
        hr, hi = carry
        xr = jnp.einsum('btgh,gph->btgp', uc, bbr)
        xi = jnp.einsum('btgh,gph->btgp', uc, bbi)
        xr = xr.at[:, 0].add(ar * hr - ai * hi)
        xi = xi.at[:, 0].add(ar * hi + ai * hr)
        shp = xr.shape
        _, _, sr, si = lax.associative_scan(
            combine, (jnp.broadcast_to(ar, shp), jnp.broadcast_to(ai, shp), xr, xi), axis=1)
        y = jnp.einsum('btgp,ghp->btgh', sr, c_re) - jnp.einsum('btgp,ghp->btgh', si, c_im)
        return (sr[:, -1], si[:, -1]), y

    (hr, hi), y = lax.scan(step, (h_re.astype(F32), h_im.astype(F32)), ug)
    y = from_chunks(y).reshape(bt, n, S5_WIDTH) + d_skip.astype(F32) * uf
    a = jax.nn.gelu(y)
    out = a * jax.nn.sigmoid(a @ glu_w.astype(F32) + glu_b.astype(F32))
    return out.astype(u.dtype), hr, hi


def ssd_mixer(z, xbc, dt_raw, h, conv_buf, conv_w, conv_b, dt_bias, a_log, d_skip, norm_w):
    bt, n, _ = z.shape
    xbc, new_buf = causal_dwconv(xbc, conv_buf, conv_w, conv_b)
    xbc = jax.nn.silu(xbc.astype(F32))
    xs, bmat, cmat = jnp.split(xbc, [SSD_INNER, SSD_INNER + SSD_GROUPS * SSD_STATE], axis=-1)
    xs = xs.reshape(bt, n, SSD_GROUPS, SSD_HPG, SSD_HEAD_DIM)
    bmat = bmat.reshape(bt, n, SSD_GROUPS, SSD_STATE)
    cmat = cmat.reshape(bt, n, SSD_GROUPS, SSD_STATE)
    dt = jax.nn.softplus(dt_raw.astype(F32) + dt_bias.astype(F32)).reshape(bt, n, SSD_GROUPS, SSD_HPG)
    la = dt * (-jnp.exp(a_log.astype(F32))).reshape(SSD_GROUPS, SSD_HPG)
    t = min(CHUNK, n)
    mask = jnp.tril(jnp.ones((t, t), dtype=bool))[None, :, :, None, None]

    def step(hc, inp):
        xc, bc, cc, dtc, lac = inp
        cum = jnp.cumsum(lac, axis=1)
        seg = cum[:, :, None] - cum[:, None, :]
        decay = jnp.exp(jnp.where(mask, seg, -jnp.inf))
        scores = jnp.einsum('btgn,bsgn->btsg', cc, bc)
        y = jnp.einsum('btsg,btsgr,bsgrp->btgrp', scores, decay, dtc[..., None] * xc)
        y = y + jnp.einsum('btgn,bgrpn->btgrp', cc, hc) * jnp.exp(cum)[..., None]
        last = cum[:, -1]
        wgt = jnp.exp(last[:, None] - cum) * dtc
        hn = hc * jnp.exp(last)[..., None, None] + jnp.einsum('bsgn,bsgr,bsgrp->bgrpn', bc, wgt, xc)
        return hn, y

    h0 = h.astype(F32).reshape(bt, SSD_GROUPS, SSD_HPG, SSD_HEAD_DIM, SSD_STATE)
    hN, y = lax.scan(step, h0, (to_chunks(xs, t), to_chunks(bmat, t), to_chunks(cmat, t),
                                to_chunks(dt, t), to_chunks(la, t)))
    y = from_chunks(y) + d_skip.astype(F32).reshape(SSD_GROUPS, SSD_HPG)[..., None] * xs
    y = y.reshape(bt, n, SSD_INNER)
    y = rmsnorm(y * jax.nn.silu(z.astype(F32)), norm_w)
    return y.astype(z.dtype), hN.reshape(bt, SSD_HEADS, SSD_HEAD_DIM, SSD_STATE), new_buf


def hgrn2_mixer(q, fz, iv, g, s, lb, norm_w):
    bt, n, _ = q.shape
    shp = (bt, n, HG_HEADS, HG_EXPAND)
    q = q.astype(F32).reshape(shp)
    fz = fz.astype(F32).reshape(shp)
    iv = iv.astype(F32).reshape(shp)
    lbh = lb.reshape(HG_HEADS, HG_EXPAND)
    log_f = jnp.logaddexp(jax.nn.log_sigmoid(fz), jnp.log(lbh) + jax.nn.log_sigmoid(-fz))
    k = (1.0 - lbh) * jax.nn.sigmoid(-fz)
    t = min(CHUNK, n)
    mask = jnp.tril(jnp.ones((t, t), dtype=bool))[None, :, :, None, None]

    def step(sc, inp):
        qc, kc, ic, lfc = inp
        cum = jnp.cumsum(lfc, axis=1)
        seg = cum[:, :, None] - cum[:, None, :]
        decay = jnp.exp(jnp.where(mask, seg, -jnp.inf))
        att = jnp.einsum('bthk,btshk,bshk->btsh', qc, decay, kc)
        o = jnp.einsum('btsh,bshv->bthv', att, ic) + jnp.einsum('bthk,bhkv->bthv', qc * jnp.exp(cum), sc)
        last = cum[:, -1]
        sn = sc * jnp.exp(last)[..., None] + jnp.einsum('bshk,bshv->bhkv', kc * jnp.exp(last[:, None] - cum), ic)
        return sn, o

    sN, o = lax.scan(step, s.astype(F32), (to_chunks(q, t), to_chunks(k, t), to_chunks(iv, t), to_chunks(log_f, t)))
    o = rmsnorm(from_chunks(o), norm_w).reshape(bt, n, HG_WIDTH)
    o = o * jax.nn.silu(g.astype(F32))
    return o.astype(g.dtype), sN


def conv_ffn(x, buf, w_up, conv_w, conv_b, w_down):
    up = x @ w_up
    a, v = jnp.split(up, 2, axis=-1)
    a, new_buf = causal_dwconv(a, buf, conv_w, conv_b)
    return (jax.nn.silu(a) * v) @ w_down, new_buf


def run_trunk(x, s5_re, s5_im, ssd_h, ssd_buf, hg_s, ffn_buf, w):
    bt, n, _ = x.shape
    outs = ([], [], [], [], [], [])
    for l in range(DEPTH):
        hn = rmsnorm(x, w['norm_mix_w'][l])
        u_a, z_b, xbc_b, dt_b, q_c, f_c, i_c, g_c, gate_logits = jnp.split(hn @ w['w_in'][l], IN_SPLITS, axis=-1)
        ya, hr, hi = s5_mixer(u_a, s5_re[l], s5_im[l], w['s5_lambda_re'][l], w['s5_lambda_im'][l],
                              w['s5_log_dt'][l], w['s5_b_re'][l], w['s5_b_im'][l], w['s5_c_re'][l],
                              w['s5_c_im'][l], w['s5_d'][l], w['s5_glu_w'][l], w['s5_glu_b'][l])
        yb, hb, bufb = ssd_mixer(z_b, xbc_b, dt_b, ssd_h[l], ssd_buf[l], w['ssd_conv_w'][l], w['ssd_conv_b'][l],
                                 w['ssd_dt_bias'][l], w['ssd_a_log'][l], w['ssd_d'][l], w['ssd_norm_w'][l])
        yc, sc = hgrn2_mixer(q_c, f_c, i_c, g_c, hg_s[l], w['hg_lb'][l], w['hg_norm_w'][l])
        gates = jax.nn.sigmoid(gate_logits.astype(F32)).reshape(bt, n, N_BRANCH, D_MODEL).astype(x.dtype)
        merged = (gates[:, :, 0] * (ya @ w['w_branch_a'][l])
                  + gates[:, :, 1] * (yb @ w['w_branch_b'][l])
                  + gates[:, :, 2] * (yc @ w['w_branch_c'][l]))
        x = x + merged @ w['w_out'][l]
        f_out, fb = conv_ffn(rmsnorm(x, w['norm_ffn_w'][l]), ffn_buf[l], w['ffn_w_up'][l],
                             w['ffn_conv_w'][l], w['ffn_conv_b'][l], w['ffn_w_down'][l])
        x = x + f_out
        for lst, val in zip(outs, (hr, hi, hb, bufb, sc, fb)):
            lst.append(val)
    y = rmsnorm(x, w['norm_final_w'])
    new = tuple(jnp.stack(v).astype(x.dtype) for v in outs)
    return (y,) + new


def setup_inputs(seed: int = 0) -> dict:
    key = jax.random.key(seed)
    kit = iter(jax.random.split(key, 48))

    def nrm(shape, scale):
        return scale * jax.random.normal(next(kit), shape, F32)

    def uni(shape, lo, hi):
        return jax.random.uniform(next(kit), shape, F32, lo, hi)

    L = DEPTH
    d = {}
    d['x_prompt'] = nrm((BATCH, SEQ, D_MODEL), 1.0)
    d['x_sample'] = nrm((DEC_BATCH, DEC_SEQ, D_MODEL), 1.0)
    d['state_s5_re'] = nrm((L, DEC_BATCH, S5_GROUPS, S5_STATE), 0.1)
    d['state_s5_im'] = nrm((L, DEC_BATCH, S5_GROUPS, S5_STATE), 0.1)
    d['state_ssd'] = nrm((L, DEC_BATCH, SSD_HEADS, SSD_HEAD_DIM, SSD_STATE), 0.1)
    d['cache_ssd_conv'] = nrm((L, DEC_BATCH, SSD_CONV - 1, SSD_CONV_DIM), 1.0)
    d['state_hgrn'] = nrm((L, DEC_BATCH, HG_HEADS, HG_EXPAND, HG_EXPAND), 0.1)
    d['cache_ffn_conv'] = nrm((L, DEC_BATCH, FFN_CONV - 1, FFN_DIM), 0.5)
    d['norm_mix_w'] = 1.0 + nrm((L, D_MODEL), 0.01)
    d['w_in'] = nrm((L, D_MODEL, IN_TOTAL), D_MODEL ** -0.5)
    d['s5_lambda_re'] = -0.5 + nrm((L, S5_GROUPS, S5_STATE), 0.01)
    d['s5_lambda_im'] = jnp.pi * jnp.arange(S5_STATE, dtype=F32) + nrm((L, S5_GROUPS, S5_STATE), 0.01)
    d['s5_log_dt'] = uni((L, S5_GROUPS), float(np.log(1e-3)), float(np.log(1e-1)))
    d['s5_b_re'] = nrm((L, S5_GROUPS, S5_STATE, S5_GROUP), (2 * S5_GROUP) ** -0.5)
    d['s5_b_im'] = nrm((L, S5_GROUPS, S5_STATE, S5_GROUP), (2 * S5_GROUP) ** -0.5)
    d['s5_c_re'] = nrm((L, S5_GROUPS, S5_GROUP, S5_STATE), 0.5)
    d['s5_c_im'] = nrm((L, S5_GROUPS, S5_GROUP, S5_STATE), 0.5)
    d['s5_d'] = 1.0 + nrm((L, S5_WIDTH), 0.1)
    d['s5_glu_w'] = nrm((L, S5_WIDTH, S5_WIDTH), S5_WIDTH ** -0.5)
    d['s5_glu_b'] = nrm((L, S5_WIDTH), 0.01)
    d['ssd_conv_w'] = nrm((L, SSD_CONV, SSD_CONV_DIM), SSD_CONV ** -0.5)
    d['ssd_conv_b'] = nrm((L, SSD_CONV_DIM), 0.01)
    dt0 = jnp.exp(uni((L, SSD_HEADS), float(np.log(1e-3)), float(np.log(1e-1))))
    d['ssd_dt_bias'] = dt0 + jnp.log(-jnp.expm1(-dt0))
    d['ssd_a_log'] = jnp.log(uni((L, SSD_HEADS), 1.0, 16.0))
    d['ssd_d'] = 1.0 + nrm((L, SSD_HEADS), 0.01)
    d['ssd_norm_w'] = 1.0 + nrm((L, SSD_INNER), 0.01)
    d['hg_lb_logits'] = nrm((L, HG_WIDTH), 0.1)
    d['hg_norm_w'] = 1.0 + nrm((L, HG_EXPAND), 0.01)
    d['w_branch_a'] = nrm((L, S5_WIDTH, D_MODEL), S5_WIDTH ** -0.5)
    d['w_branch_b'] = nrm((L, SSD_INNER, D_MODEL), SSD_INNER ** -0.5)
    d['w_branch_c'] = nrm((L, HG_WIDTH, D_MODEL), HG_WIDTH ** -0.5)
    d['w_out'] = nrm((L, D_MODEL, D_MODEL), D_MODEL ** -0.5)
    d['norm_ffn_w'] = 1.0 + nrm((L, D_MODEL), 0.01)
    d['ffn_w_up'] = nrm((L, D_MODEL, 2 * FFN_DIM), D_MODEL ** -0.5)
    d['ffn_conv_w'] = nrm((L, FFN_CONV, FFN_DIM), FFN_CONV ** -0.5)
    d['ffn_conv_b'] = nrm((L, FFN_DIM), 0.01)
    d['ffn_w_down'] = nrm((L, FFN_DIM, D_MODEL), FFN_DIM ** -0.5)
    d['norm_final_w'] = 1.0 + nrm((D_MODEL,), 0.01)
    return d


def reference(x_prompt, x_sample, state_s5_re, state_s5_im, state_ssd, cache_ssd_conv, state_hgrn, cache_ffn_conv,
              norm_mix_w, w_in, s5_lambda_re, s5_lambda_im, s5_log_dt, s5_b_re, s5_b_im, s5_c_re, s5_c_im,
              s5_d, s5_glu_w, s5_glu_b, ssd_conv_w, ssd_conv_b, ssd_dt_bias, ssd_a_log, ssd_d, ssd_norm_w,
              hg_lb_logits, hg_norm_w, w_branch_a, w_branch_b, w_branch_c, w_out,
              norm_ffn_w, ffn_w_up, ffn_conv_w, ffn_conv_b, ffn_w_down, norm_final_w):
    lb_cum = jnp.cumsum(jax.nn.softmax(hg_lb_logits.astype(F32), axis=0), axis=0)
    hg_lb = lb_cum - lb_cum[0]
    w = dict(norm_mix_w=norm_mix_w, w_in=w_in, s5_lambda_re=s5_lambda_re, s5_lambda_im=s5_lambda_im,
             s5_log_dt=s5_log_dt, s5_b_re=s5_b_re, s5_b_im=s5_b_im, s5_c_re=s5_c_re, s5_c_im=s5_c_im,
             s5_d=s5_d, s5_glu_w=s5_glu_w, s5_glu_b=s5_glu_b, ssd_conv_w=ssd_conv_w, ssd_conv_b=ssd_conv_b,
             ssd_dt_bias=ssd_dt_bias, ssd_a_log=ssd_a_log, ssd_d=ssd_d, ssd_norm_w=ssd_norm_w,
             hg_lb=hg_lb, hg_norm_w=hg_norm_w, w_branch_a=w_branch_a, w_branch_b=w_branch_b,
             w_branch_c=w_branch_c, w_out=w_out, norm_ffn_w=norm_ffn_w, ffn_w_up=ffn_w_up,
             ffn_conv_w=ffn_conv_w, ffn_conv_b=ffn_conv_b, ffn_w_down=ffn_w_down, norm_final_w=norm_final_w)
    bp = x_prompt.shape[0]
    dtp = x_prompt.dtype
    (y_prompt, p_s5_re, p_s5_im, p_ssd, p_ssd_conv, p_hgrn, p_ffn_conv) = run_trunk(
        x_prompt,
        jnp.zeros((DEPTH, bp, S5_GROUPS, S5_STATE), dtp),
        jnp.zeros((DEPTH, bp, S5_GROUPS, S5_STATE), dtp),
        jnp.zeros((DEPTH, bp, SSD_HEADS, SSD_HEAD_DIM, SSD_STATE), dtp),
        jnp.zeros((DEPTH, bp, SSD_CONV - 1, SSD_CONV_DIM), dtp),
        jnp.zeros((DEPTH, bp, HG_HEADS, HG_EXPAND, HG_EXPAND), dtp),
        jnp.zeros((DEPTH, bp, FFN_CONV - 1, FFN_DIM), dtp),
        w)
    (y_sample, s_s5_re, s_s5_im, s_ssd, s_ssd_conv, s_hgrn, s_ffn_conv) = run_trunk(
        x_sample, state_s5_re, state_s5_im, state_ssd, cache_ssd_conv, state_hgrn, cache_ffn_conv, w)
    return (y_prompt, y_sample, p_s5_re, p_s5_im, p_ssd, p_ssd_conv, p_hgrn, p_ffn_conv,
            s_s5_re, s_s5_im, s_ssd, s_ssd_conv, s_hgrn, s_ffn_conv)
```

```python
import functools

import jax
import jax.numpy as jnp
from jax import lax
from jax.experimental import pallas as pl
from jax.experimental.pallas import tpu as pltpu

F32 = jnp.float32
BF16 = jnp.bfloat16
EPS = 1e-6

D_MODEL = 1024
DEPTH = 2
S5_WIDTH = 512
S5_GROUP = 16
S5_GROUPS = 32
S5_STATE = 64
S5_SLABS = 4
SSD_INNER = 1024
SSD_HEAD_DIM = 64
SSD_HEADS = 16
SSD_GROUPS = 4
SSD_HPG = 4
SSD_STATE = 128
SSD_CONV = 4
SSD_CONV_DIM = 2048
HG_WIDTH = 512
HG_EXPAND = 128
HG_HEADS = 4
HG_SUB = 16
FFN_DIM = 2816
FFN_CONV = 3
FFN_COL = 256
LANES = 128
HALO = 8
ROW_TILE = 512
VMEM_LIMIT = 56 * 1024 * 1024


def _cparams(*sem):
    return pltpu.CompilerParams(dimension_semantics=sem, vmem_limit_bytes=VMEM_LIMIT)


def _resident(shape):
    n = len(shape)
    return pl.BlockSpec(shape, lambda *_: (0,) * n, pipeline_mode=pl.Buffered(1))


def _rms(x, w):
    return x * lax.rsqrt(jnp.mean(x * x, axis=-1, keepdims=True) + EPS) * w


def _silu(x):
    return x * jax.nn.sigmoid(x)


def _dot(a, b):
    return jnp.dot(a, b, preferred_element_type=F32)


def _dot_nt(a, b):
    return lax.dot_general(a, b, (((1,), (1,)), ((), ())), preferred_element_type=F32)


def _dot_tn(a, b):
    return lax.dot_general(a, b, (((0,), (0,)), ((), ())), preferred_element_type=F32)


def _cumsum_rows(tri, x):
    return jnp.dot(tri, x, preferred_element_type=F32, precision=lax.Precision.HIGHEST)


def _tri(t):
    r = lax.broadcasted_iota(jnp.int32, (t, t), 0)
    c = lax.broadcasted_iota(jnp.int32, (t, t), 1)
    return r >= c


_IN_PIECES = (
    ("u", S5_WIDTH, F32), ("z", SSD_INNER, BF16), ("xbc", SSD_CONV_DIM, BF16),
    ("q", HG_WIDTH, BF16), ("f", HG_WIDTH, F32), ("i", HG_WIDTH, BF16), ("g", HG_WIDTH, BF16),
    ("gates", 3 * D_MODEL, BF16), ("dt", LANES, F32),
)
_IN_COLS = sum(p[1] for p in _IN_PIECES)


def _inproj_kernel(x_ref, nw_ref, w_ref, *out_refs):
    hn = _rms(x_ref[...], nw_ref[...]).astype(BF16)
    off = 0
    for (_, width, dtype), o_ref in zip(_IN_PIECES, out_refs):
        step = min(width, 512)
        for c in range(0, width, step):
            o_ref[:, c:c + step] = _dot(hn, w_ref[:, off + c:off + c + step]).astype(dtype)
        off += width


def _inproj(x2d, norm_w, w_perm):
    m = x2d.shape[0]
    tm = min(m, ROW_TILE)
    outs = tuple(jax.ShapeDtypeStruct((m, width), dtype) for _, width, dtype in _IN_PIECES)
    return pl.pallas_call(
        _inproj_kernel,
        grid=(m // tm,),
        in_specs=[pl.BlockSpec((tm, D_MODEL), lambda i: (i, 0)),
                  _resident((1, D_MODEL)), _resident((D_MODEL, _IN_COLS))],
        out_specs=tuple(pl.BlockSpec((tm, width), lambda i: (i, 0)) for _, width, _ in _IN_PIECES),
        out_shape=outs,
        compiler_params=_cparams("parallel"),
        name="inproj",
    )(x2d, norm_w, w_perm)


def _gelu_tanh(x):
    c = 0.7978845608028654
    return 0.5 * x * (1.0 + jnp.tanh(c * (x + 0.044715 * (x * x * x))))


def _s5_kernel(u_ref, h0r_ref, h0i_ref, ar_ref, ai_ref, bbd_ref, cbd_ref, d_ref, gw_ref, gb_ref,
               o_ref, hr_ref, hi_ref, utm, xs, ytm, st):
    nb, tt, _ = u_ref.shape
    half = S5_STATE * S5_GROUPS // S5_SLABS

    @pl.when(pl.program_id(0) == 0)
    def _():
        for j in range(S5_SLABS):
            st[2 * j] = h0r_ref[:, j * half:(j + 1) * half]
            st[2 * j + 1] = h0i_ref[:, j * half:(j + 1) * half]

    def to_time_major(t, c):
        utm[t] = u_ref[:, t, :]
        return c
    lax.fori_loop(0, tt, to_time_major, 0)

    u2 = utm[...].reshape(tt * nb, S5_WIDTH)
    for j in range(S5_SLABS):
        xs[...] = _dot(u2[:, j * LANES:(j + 1) * LANES].astype(BF16), bbd_ref[j])
        ar = jnp.broadcast_to(ar_ref[:, j * half:(j + 1) * half], (nb, half))
        ai = jnp.broadcast_to(ai_ref[:, j * half:(j + 1) * half], (nb, half))

        def scan_step(t, carry):
            sr, si = carry
            rows = pl.ds(pl.multiple_of(t * nb, nb), nb)
            nr = ar * sr - ai * si + xs[rows, 0:half]
            ni = ar * si + ai * sr + xs[rows, half:2 * half]
            xs[rows, 0:half] = nr
            xs[rows, half:2 * half] = ni
            return nr, ni
        sr, si = lax.fori_loop(0, tt, scan_step, (st[2 * j], st[2 * j + 1]))
        st[2 * j] = sr
        st[2 * j + 1] = si
        hr_ref[:, j * half:(j + 1) * half] = sr
        hi_ref[:, j * half:(j + 1) * half] = si
        ytm[:, j * LANES:(j + 1) * LANES] = _dot(xs[...].astype(BF16), cbd_ref[j])

    y = ytm[...] + d_ref[...] * u2
    a = _gelu_tanh(y)
    out = a * jax.nn.sigmoid(_dot(a.astype(BF16), gw_ref[...]) + gb_ref[...])
    utm[...] = out.reshape(tt, nb, S5_WIDTH)
    for b in range(nb):
        o_ref[b] = utm[:, b, :].astype(o_ref.dtype)


def _s5_mixer(u, h0r, h0i, prm):
    b, n, _ = u.shape
    tt = ROW_TILE // b
    nst = S5_GROUPS * S5_STATE
    half = nst // S5_SLABS
    return pl.pallas_call(
        _s5_kernel,
        grid=(n // tt,),
        in_specs=[pl.BlockSpec((b, tt, S5_WIDTH), lambda t: (0, t, 0)),
                  _resident((b, nst)), _resident((b, nst)),
                  _resident((1, nst)), _resident((1, nst)),
                  _resident((S5_SLABS, LANES, 2 * half)), _resident((S5_SLABS, 2 * half, LANES)),
                  _resident((1, S5_WIDTH)), _resident((S5_WIDTH, S5_WIDTH)), _resident((1, S5_WIDTH))],
        out_specs=(pl.BlockSpec((b, tt, S5_WIDTH), lambda t: (0, t, 0)),
                   pl.BlockSpec((b, nst), lambda t: (0, 0)),
                   pl.BlockSpec((b, nst), lambda t: (0, 0))),
        out_shape=(jax.ShapeDtypeStruct((b, n, S5_WIDTH), BF16),
                   jax.ShapeDtypeStruct((b, nst), F32),
                   jax.ShapeDtypeStruct((b, nst), F32)),
        scratch_shapes=[pltpu.VMEM((tt, b, S5_WIDTH), F32),
                        pltpu.VMEM((tt * b, 2 * half), F32),
                        pltpu.VMEM((tt * b, S5_WIDTH), F32),
                        pltpu.VMEM((2 * S5_SLABS, b, half), F32)],
        compiler_params=_cparams("arbitrary"),
        name="s5_mixer",
    )(u, h0r, h0i, prm["ar"], prm["ai"], prm["bbd"], prm["cbd"], prm["d"], prm["glu_w"], prm["glu_b"])


def _s5_params(lam_re, lam_im, log_dt, b_re, b_im, c_re, c_im, d_skip, glu_w, glu_b):
    dt = jnp.exp(log_dt)[:, None]
    mag = jnp.exp(lam_re * dt)
    ar, ai = mag * jnp.cos(lam_im * dt), mag * jnp.sin(lam_im * dt)
    den = lam_re * lam_re + lam_im * lam_im
    nr, ni = ar - 1.0, ai
    cr = (nr * lam_re + ni * lam_im) / den
    ci = (ni * lam_re - nr * lam_im) / den
    bbr = cr[..., None] * b_re - ci[..., None] * b_im
    bbi = cr[..., None] * b_im + ci[..., None] * b_re
    gps = S5_GROUPS // S5_SLABS
    eye = jnp.eye(gps, dtype=F32)

    def pack_b(bb):
        bb = bb.reshape(S5_SLABS, gps, S5_STATE, S5_GROUP)
        return jnp.einsum("sgph,gk->sghkp", bb, eye).reshape(S5_SLABS, gps * S5_GROUP, gps * S5_STATE)

    def pack_c(cc):
        cc = cc.reshape(S5_SLABS, gps, S5_GROUP, S5_STATE)
        return jnp.einsum("sghp,gk->sgpkh", cc, eye).reshape(S5_SLABS, gps * S5_STATE, gps * S5_GROUP)

    bbd = jnp.concatenate([pack_b(bbr), pack_b(bbi)], axis=2).astype(BF16)
    cbd = jnp.concatenate([pack_c(c_re), -pack_c(c_im)], axis=1).astype(BF16)
    return dict(ar=ar.reshape(1, -1), ai=ai.reshape(1, -1), bbd=bbd, cbd=cbd,
                d=d_skip.reshape(1, -1), glu_w=glu_w.astype(BF16), glu_b=glu_b.reshape(1, -1))


def _ssd_kernel(z_ref, xbc_ref, dt_ref, h0_ref, cache_ref, cw_ref, cb_ref, dtb_ref, alog_ref, dsk_ref, nw_ref,
                o_ref, hn_ref, buf_ref, cbuf, ht, ybuf, xwbuf):
    t = z_ref.shape[1]
    c = pl.program_id(1)
    gs = SSD_HPG * SSD_HEAD_DIM

    @pl.when(c == 0)
    def _():
        cbuf[0:HALO, :] = cache_ref[0]
        for g in range(SSD_GROUPS):
            ht[:, g * gs:(g + 1) * gs] = h0_ref[0, g * gs:(g + 1) * gs, :].T

    cbuf[HALO:HALO + t, :] = xbc_ref[0].astype(F32)
    acc = cb_ref[...] + cw_ref[SSD_CONV - 1:SSD_CONV, :] * cbuf[HALO:HALO + t, :]
    for j in range(SSD_CONV - 1):
        off = HALO - (SSD_CONV - 1) + j
        acc = acc + cw_ref[j:j + 1, :] * cbuf[off:off + t, :]
    tail = cbuf[t:t + HALO, :]
    cbuf[0:HALO, :] = tail
    buf_ref[0] = tail
    xbc = _silu(acc)
    xs = xbc[:, :SSD_INNER]
    bmat = xbc[:, SSD_INNER:SSD_INNER + SSD_GROUPS * SSD_STATE].astype(BF16)
    cmat = xbc[:, SSD_INNER + SSD_GROUPS * SSD_STATE:].astype(BF16)

    dt = jax.nn.softplus(dt_ref[0] + dtb_ref[...])
    la = dt * (-jnp.exp(alog_ref[...]))
    mask = _tri(t)
    cum = _cumsum_rows(mask.astype(F32), la)
    cum_t = cum.T
    last = cum[t - 1:t, :]
    ecum = jnp.exp(cum)
    wgt = jnp.exp(last - cum) * dt
    elast = jnp.exp(last)

    for g in range(SSD_GROUPS):
        bg = bmat[:, g * SSD_STATE:(g + 1) * SSD_STATE]
        cg = cmat[:, g * SSD_STATE:(g + 1) * SSD_STATE]
        scores = _dot_nt(cg, bg)
        hg = ht[:, g * gs:(g + 1) * gs]
        yint = _dot(cg, hg.astype(BF16))
        for r in range(SSD_HPG):
            h = g * SSD_HPG + r
            xh = xs[:, h * SSD_HEAD_DIM:(h + 1) * SSD_HEAD_DIM]
            seg = cum[:, h:h + 1] - cum_t[h:h + 1, :]
            decay = jnp.exp(jnp.where(mask, seg, -jnp.inf))
            m = (scores * decay).astype(BF16)
            y = _dot(m, (dt[:, h:h + 1] * xh).astype(BF16))
            y = y + yint[:, r * SSD_HEAD_DIM:(r + 1) * SSD_HEAD_DIM] * ecum[:, h:h + 1]
            ybuf[:, h * SSD_HEAD_DIM:(h + 1) * SSD_HEAD_DIM] = y
            xwbuf[:, h * SSD_HEAD_DIM:(h + 1) * SSD_HEAD_DIM] = (wgt[:, h:h + 1] * xh).astype(BF16)
            ht[:, h * SSD_HEAD_DIM:(h + 1) * SSD_HEAD_DIM] = (
                hg[:, r * SSD_HEAD_DIM:(r + 1) * SSD_HEAD_DIM] * elast[:, h:h + 1])
        ht[:, g * gs:(g + 1) * gs] += _dot_tn(bg, xwbuf[:, g * gs:(g + 1) * gs])

    y = ybuf[...] + dsk_ref[...] * xs
    y = _rms(y * _silu(z_ref[0].astype(F32)), nw_ref[...])
    o_ref[0] = y.astype(o_ref.dtype)

    @pl.when(c == pl.num_programs(1) - 1)
    def _():
        for g in range(SSD_GROUPS):
            hn_ref[0, g * gs:(g + 1) * gs, :] = ht[:, g * gs:(g + 1) * gs].T


def _ssd_mixer(z, xbc, dt, h0, cache, prm):
    b, n, _ = z.shape
    t = min(n, 128)
    rows = SSD_HEADS * SSD_HEAD_DIM
    return pl.pallas_call(
        _ssd_kernel,
        grid=(b, n // t),
        in_specs=[pl.BlockSpec((1, t, SSD_INNER), lambda i, c: (i, c, 0)),
                  pl.BlockSpec((1, t, SSD_CONV_DIM), lambda i, c: (i, c, 0)),
                  pl.BlockSpec((1, t, LANES), lambda i, c: (i, c, 0)),
                  pl.BlockSpec((1, rows, SSD_STATE), lambda i, c: (i, 0, 0)),
                  pl.BlockSpec((1, HALO, SSD_CONV_DIM), lambda i, c: (i, 0, 0)),
                  _resident((SSD_CONV, SSD_CONV_DIM)), _resident((1, SSD_CONV_DIM)),
                  _resident((1, LANES)), _resident((1, LANES)),
                  _resident((1, SSD_INNER)), _resident((1, SSD_INNER))],
        out_specs=(pl.BlockSpec((1, t, SSD_INNER), lambda i, c: (i, c, 0)),
                   pl.BlockSpec((1, rows, SSD_STATE), lambda i, c: (i, 0, 0)),
                   pl.BlockSpec((1, HALO, SSD_CONV_DIM), lambda i, c: (i, 0, 0))),
        out_shape=(jax.ShapeDtypeStruct((b, n, SSD_INNER), BF16),
                   jax.ShapeDtypeStruct((b, rows, SSD_STATE), F32),
                   jax.ShapeDtypeStruct((b, HALO, SSD_CONV_DIM), F32)),
        scratch_shapes=[pltpu.VMEM((t + HALO, SSD_CONV_DIM), F32),
                        pltpu.VMEM((SSD_STATE, rows), F32),
                        pltpu.VMEM((t, SSD_INNER), F32),
                        pltpu.VMEM((t, SSD_INNER), BF16)],
        compiler_params=_cparams("parallel", "arbitrary"),
        name="ssd_mixer",
    )(z, xbc, dt, h0, cache, prm["conv_w"], prm["conv_b"], prm["dt_bias"], prm["a_log"], prm["d"], prm["norm_w"])


def _pad_lanes(v):
    return jnp.pad(v, (0, LANES - v.shape[0])).reshape(1, LANES)


def _ssd_params(conv_w, conv_b, dt_bias, a_log, d_skip, norm_w):
    return dict(conv_w=conv_w, conv_b=conv_b.reshape(1, -1), dt_bias=_pad_lanes(dt_bias), a_log=_pad_lanes(a_log),
                d=jnp.repeat(d_skip, SSD_HEAD_DIM).reshape(1, -1), norm_w=norm_w.reshape(1, -1))


def _hgrn_kernel(q_ref, f_ref, i_ref, g_ref, s0_ref, lb_ref, nw_ref, o_ref, sn_ref, st, att_s):
    t = q_ref.shape[1]
    c = pl.program_id(1)
    nsub = t // HG_SUB
    k = HG_EXPAND

    @pl.when(c == 0)
    def _():
        for h in range(HG_HEADS):
            st[h] = s0_ref[0, h * k:(h + 1) * k, :].T

    q = q_ref[0].astype(F32)
    fz = f_ref[0]
    iv = i_ref[0]
    lb = lb_ref[...]
    lsp = jnp.minimum(fz, 0.0) - jnp.log1p(jnp.exp(-jnp.abs(fz)))
    lsn = lsp - fz
    lbt = jnp.log(lb) + lsn
    log_f = jnp.maximum(lsp, lbt) + jnp.log1p(jnp.exp(-jnp.abs(lsp - lbt)))
    kk = (1.0 - lb) * jax.nn.sigmoid(-fz)

    cum = _cumsum_rows(_tri(t).astype(F32), log_f)
    last = cum[t - 1:t, :]
    qe = (q * jnp.exp(cum)).astype(BF16)
    ke = (kk * jnp.exp(last - cum)).astype(BF16)
    elast = jnp.exp(last)

    qloc, kloc, refs = [], [], []
    for m in range(nsub):
        rows = slice(m * HG_SUB, (m + 1) * HG_SUB)
        r_in = cum[m * HG_SUB - 1:m * HG_SUB, :] if m else None
        r_out = cum[(m + 1) * HG_SUB - 1:(m + 1) * HG_SUB, :]
        qloc.append(q[rows] * jnp.exp(cum[rows] - r_in) if m else None)
        kloc.append((kk[rows] * jnp.exp(r_out - cum[rows])).astype(BF16))
        refs.append((r_in, r_out))
    kl = jnp.concatenate(kloc, axis=0)
    pairs = [(i, j) for i in range(1, nsub) for j in range(i)]
    if pairs:
        lhs = jnp.concatenate(
            [(qloc[i] * jnp.exp(refs[i][0] - refs[j][1])).astype(BF16) for i, j in pairs], axis=0)
    col = lax.broadcasted_iota(jnp.int32, (HG_SUB, t), 1)
    colblk = col // HG_SUB
    row1 = lax.broadcasted_iota(jnp.int32, (HG_SUB, 1), 0)

    for h in range(HG_HEADS):
        hs = slice(h * k, (h + 1) * k)
        if pairs:
            blk = _dot_nt(lhs[:, hs], kl[:, hs])
        for i in range(nsub):
            acc = jnp.zeros((HG_SUB, t), F32)
            for p, (pi, pj) in enumerate(pairs):
                if pi == i:
                    acc = jnp.where(colblk == pj, blk[p * HG_SUB:(p + 1) * HG_SUB, :], acc)
            att_s[h, i * HG_SUB:(i + 1) * HG_SUB, :] = acc

    for i in range(nsub):
        rows = slice(i * HG_SUB, (i + 1) * HG_SUB)
        qi = q[rows]
        ci = cum[rows]
        accs = [att_s[h, rows, :] for h in range(HG_HEADS)]
        for s in range(HG_SUB):
            gs = i * HG_SUB + s
            p = qi * jnp.exp(jnp.minimum(ci - cum[gs:gs + 1, :], 0.0)) * kk[gs:gs + 1, :]
            for h in range(HG_HEADS):
                a = jnp.sum(p[:, h * k:(h + 1) * k], axis=-1, keepdims=True)
                accs[h] = jnp.where(col == gs, jnp.where(row1 >= s, a, 0.0), accs[h])
        for h in range(HG_HEADS):
            att_s[h, rows, :] = accs[h]

    for h in range(HG_HEADS):
        hs = slice(h * k, (h + 1) * k)
        sh = st[h]
        o = _dot(att_s[h].astype(BF16), iv[:, hs]) + _dot_nt(qe[:, hs], sh.astype(BF16))
        st[h] = sh * elast[:, hs] + _dot_tn(iv[:, hs], ke[:, hs])
        o = _rms(o, nw_ref[...])
        o_ref[0, :, hs] = (o * _silu(g_ref[0, :, hs].astype(F32))).astype(o_ref.dtype)

    @pl.when(c == pl.num_programs(1) - 1)
    def _():
        for h in range(HG_HEADS):
            sn_ref[0, h * k:(h + 1) * k, :] = st[h].T


def _hgrn_mixer(q, f, iv, g, s0, lb, norm_w):
    b, n, _ = q.shape
    t = min(n, 64)
    rows = HG_HEADS * HG_EXPAND
    tok = lambda: pl.BlockSpec((1, t, HG_WIDTH), lambda i, c: (i, c, 0))
    return pl.pallas_call(
        _hgrn_kernel,
        grid=(b, n // t),
        in_specs=[tok(), tok(), tok(), tok(),
                  pl.BlockSpec((1, rows, HG_EXPAND), lambda i, c: (i, 0, 0)),
                  _resident((1, HG_WIDTH)), _resident((1, HG_EXPAND))],
        out_specs=(tok(), pl.BlockSpec((1, rows, HG_EXPAND), lambda i, c: (i, 0, 0))),
        out_shape=(jax.ShapeDtypeStruct((b, n, HG_WIDTH), BF16),
                   jax.ShapeDtypeStruct((b, rows, HG_EXPAND), F32)),
        scratch_shapes=[pltpu.VMEM((HG_HEADS, HG_EXPAND, HG_EXPAND), F32),
                        pltpu.VMEM((HG_HEADS, t, t), F32)],
        compiler_params=_cparams("parallel", "arbitrary"),
        name="hgrn_mixer",
    )(q, f, iv, g, s0, lb, norm_w)


def _merge_kernel(x_ref, ya_ref, yb_ref, yc_ref, gl_ref, wa_ref, wb_ref, wc_ref, wo_ref, o_ref):
    d = D_MODEL
    merged = jax.nn.sigmoid(gl_ref[:, 0:d].astype(F32)) * _dot(ya_ref[...], wa_ref[...])
    merged = merged + jax.nn.sigmoid(gl_ref[:, d:2 * d].astype(F32)) * _dot(yb_ref[...], wb_ref[...])
    merged = merged + jax.nn.sigmoid(gl_ref[:, 2 * d:3 * d].astype(F32)) * _dot(yc_ref[...], wc_ref[...])
    o_ref[...] = x_ref[...] + _dot(merged.astype(BF16), wo_ref[...])


def _merge(x2d, ya, yb, yc, gl, wa, wb, wc, wo):
    m = x2d.shape[0]
    tm = min(m, ROW_TILE)
    row = lambda w: pl.BlockSpec((tm, w), lambda i: (i, 0))
    return pl.pallas_call(
        _merge_kernel,
        grid=(m // tm,),
        in_specs=[row(D_MODEL), row(S5_WIDTH), row(SSD_INNER), row(HG_WIDTH), row(3 * D_MODEL),
                  _resident(wa.shape), _resident(wb.shape), _resident(wc.shape), _resident(wo.shape)],
        out_specs=row(D_MODEL),
        out_shape=jax.ShapeDtypeStruct((m, D_MODEL), F32),
        compiler_params=_cparams("parallel"),
        name="merge",
    )(x2d, ya, yb, yc, gl, wa, wb, wc, wo)


def _ffn_kernel(x_ref, cache_ref, nw_ref, wup_ref, cw_ref, cb_ref, wdn_ref, fw_ref,
                o_ref, buf_ref, abuf, gbuf, *, final_norm):
    nb, tt, _ = x_ref.shape

    @pl.when(pl.program_id(1) == 0)
    def _():
        abuf[:, 0:HALO, :] = cache_ref[...]

    x = x_ref[...].reshape(nb * tt, D_MODEL)
    hn = _rms(x, nw_ref[...]).astype(BF16)
    for c0 in range(0, FFN_DIM, FFN_COL):
        cs = slice(c0, c0 + FFN_COL)
        a = _dot(hn, wup_ref[:, cs])
        v = _dot(hn, wup_ref[:, FFN_DIM + c0:FFN_DIM + c0 + FFN_COL])
        abuf[:, HALO:HALO + tt, cs] = a.reshape(nb, tt, FFN_COL)
        conv = cb_ref[:, cs] + cw_ref[FFN_CONV - 1:FFN_CONV, cs] * a.reshape(nb, tt, FFN_COL)
        for j in range(FFN_CONV - 1):
            off = HALO - (FFN_CONV - 1) + j
            conv = conv + cw_ref[j:j + 1, cs] * abuf[:, off:off + tt, cs]
        gbuf[:, cs] = (_silu(conv).reshape(nb * tt, FFN_COL) * v).astype(BF16)
    tail = abuf[:, tt:tt + HALO, :]
    abuf[:, 0:HALO, :] = tail
    buf_ref[...] = tail
    y = x + _dot(gbuf[...], wdn_ref[...])
    if final_norm:
        y = _rms(y, fw_ref[...])
    o_ref[...] = y.reshape(nb, tt, D_MODEL)


def _ffn(x, cache, norm_w, w_up, conv_w, conv_b, w_down, final_w, final_norm):
    b, n, _ = x.shape
    tt = min(n, ROW_TILE)
    nb = ROW_TILE // tt
    return pl.pallas_call(
        functools.partial(_ffn_kernel, final_norm=final_norm),
        grid=(b // nb, n // tt),
        in_specs=[pl.BlockSpec((nb, tt, D_MODEL), lambda i, c: (i, c, 0)),
                  pl.BlockSpec((nb, HALO, FFN_DIM), lambda i, c: (i, 0, 0)),
                  _resident((1, D_MODEL)), _resident((D_MODEL, 2 * FFN_DIM)),
                  _resident((FFN_CONV, FFN_DIM)), _resident((1, FFN_DIM)),
                  _resident((FFN_DIM, D_MODEL)), _resident((1, D_MODEL))],
        out_specs=(pl.BlockSpec((nb, tt, D_MODEL), lambda i, c: (i, c, 0)),
                   pl.BlockSpec((nb, HALO, FFN_DIM), lambda i, c: (i, 0, 0))),
        out_shape=(jax.ShapeDtypeStruct((b, n, D_MODEL), F32),
                   jax.ShapeDtypeStruct((b, HALO, FFN_DIM), F32)),
        scratch_shapes=[pltpu.VMEM((nb, tt + HALO, FFN_DIM), F32),
                        pltpu.VMEM((nb * tt, FFN_DIM), BF16)],
        compiler_params=_cparams("parallel", "arbitrary"),
        name="ffn",
    )(x, cache, norm_w, w_up, conv_w, conv_b, w_down, final_w)


def _pad_cache(cache):
    return jnp.pad(cache, ((0, 0), (HALO - cache.shape[1], 0), (0, 0)))


def _permute_w_in(w_in):
    sizes = (S5_WIDTH, SSD_INNER, SSD_CONV_DIM, SSD_HEADS, HG_WIDTH, HG_WIDTH, HG_WIDTH, HG_WIDTH, 3 * D_MODEL)
    offs = [0]
    for s in sizes:
        offs.append(offs[-1] + s)
    u, z, xbc, dt, q, f, i, g, gates = (w_in[:, offs[k]:offs[k + 1]] for k in range(len(sizes)))
    dt = jnp.pad(dt, ((0, 0), (0, LANES - SSD_HEADS)))
    return jnp.concatenate([u, z, xbc, q, f, i, g, gates, dt], axis=1).astype(BF16)


def _run_trunk(x, s5_re, s5_im, ssd_h, ssd_buf, hg_s, ffn_buf, layers, final_w):
    b, n, _ = x.shape
    m = b * n
    outs = ([], [], [], [], [], [])
    for l, w in enumerate(layers):
        u, z, xbc, q, f, iv, g, gates, dt = _inproj(x.reshape(m, D_MODEL), w["norm_mix_w"], w["w_in"])
        r3 = lambda a: a.reshape(b, n, a.shape[-1])
        ya, hr, hi = _s5_mixer(r3(u), s5_re[l].reshape(b, -1), s5_im[l].reshape(b, -1), w["s5"])
        yb, hb, bufb = _ssd_mixer(r3(z), r3(xbc), r3(dt), ssd_h[l].reshape(b, -1, SSD_STATE),
                                  _pad_cache(ssd_buf[l]), w["ssd"])
        yc, sc = _hgrn_mixer(r3(q), r3(f), r3(iv), r3(g), hg_s[l].reshape(b, -1, HG_EXPAND),
                             w["hg_lb"], w["hg_norm_w"])
        h2d = _merge(x.reshape(m, D_MODEL), ya.reshape(m, -1), yb.reshape(m, -1), yc.reshape(m, -1), gates,
                     w["w_branch_a"], w["w_branch_b"], w["w_branch_c"], w["w_out"])
        x, fb = _ffn(h2d.reshape(b, n, D_MODEL), _pad_cache(ffn_buf[l]), w["norm_ffn_w"], w["ffn_w_up"],
                     w["ffn_conv_w"], w["ffn_conv_b"], w["ffn_w_down"], final_w, l == len(layers) - 1)
        vals = (hr.reshape(b, S5_GROUPS, S5_STATE), hi.reshape(b, S5_GROUPS, S5_STATE),
                hb.reshape(b, SSD_HEADS, SSD_HEAD_DIM, SSD_STATE), bufb[:, HALO - (SSD_CONV - 1):],
                sc.reshape(b, HG_HEADS, HG_EXPAND, HG_EXPAND), fb[:, HALO - (FFN_CONV - 1):])
        for lst, val in zip(outs, vals):
            lst.append(val)
    return (x,) + tuple(jnp.stack(v) for v in outs)


def kernel(x_prompt, x_sample, state_s5_re, state_s5_im, state_ssd, cache_ssd_conv, state_hgrn, cache_ffn_conv,
           norm_mix_w, w_in, s5_lambda_re, s5_lambda_im, s5_log_dt, s5_b_re, s5_b_im, s5_c_re, s5_c_im,
           s5_d, s5_glu_w, s5_glu_b, ssd_conv_w, ssd_conv_b, ssd_dt_bias, ssd_a_log, ssd_d, ssd_norm_w,
           hg_lb_logits, hg_norm_w, w_branch_a, w_branch_b, w_branch_c, w_out,
           norm_ffn_w, ffn_w_up, ffn_conv_w, ffn_conv_b, ffn_w_down, norm_final_w):
    lb_cum = jnp.cumsum(jax.nn.softmax(hg_lb_logits.astype(F32), axis=0), axis=0)
    hg_lb = lb_cum - lb_cum[0]
    layers = []
    for l in range(DEPTH):
        layers.append(dict(
            norm_mix_w=norm_mix_w[l].reshape(1, -1), w_in=_permute_w_in(w_in[l]),
            s5=_s5_params(s5_lambda_re[l], s5_lambda_im[l], s5_log_dt[l], s5_b_re[l], s5_b_im[l],
                          s5_c_re[l], s5_c_im[l], s5_d[l], s5_glu_w[l], s5_glu_b[l]),
            ssd=_ssd_params(ssd_conv_w[l], ssd_conv_b[l], ssd_dt_bias[l], ssd_a_log[l], ssd_d[l], ssd_norm_w[l]),
            hg_lb=hg_lb[l].reshape(1, -1), hg_norm_w=hg_norm_w[l].reshape(1, -1),
            w_branch_a=w_branch_a[l].astype(BF16), w_branch_b=w_branch_b[l].astype(BF16),
            w_branch_c=w_branch_c[l].astype(BF16), w_out=w_out[l].astype(BF16),
            norm_ffn_w=norm_ffn_w[l].reshape(1, -1), ffn_w_up=ffn_w_up[l].astype(BF16),
            ffn_conv_w=ffn_conv_w[l], ffn_conv_b=ffn_conv_b[l].reshape(1, -1),
            ffn_w_down=ffn_w_down[l].astype(BF16)))
    final_w = norm_final_w.reshape(1, -1)
    bp = x_prompt.shape[0]
    zeros = lambda *s: jnp.zeros((DEPTH, bp) + s, F32)
    p = _run_trunk(x_prompt, zeros(S5_GROUPS, S5_STATE), zeros(S5_GROUPS, S5_STATE),
                   zeros(SSD_HEADS, SSD_HEAD_DIM, SSD_STATE), zeros(SSD_CONV - 1, SSD_CONV_DIM),
                   zeros(HG_HEADS, HG_EXPAND, HG_EXPAND), zeros(FFN_CONV - 1, FFN_DIM), layers, final_w)
    s = _run_trunk(x_sample, state_s5_re, state_s5_im, state_ssd, cache_ssd_conv, state_hgrn, cache_ffn_conv,
                   layers, final_w)
    return (p[0], s[0]) + p[1:] + s[1:]
```

```python
import functools

import jax
import jax.numpy as jnp
from jax import lax
from jax.experimental import pallas as pl
from jax.experimental.pallas import tpu as pltpu

F32 = jnp.float32
BF16 = jnp.bfloat16
EPS = 1e-6

D_MODEL = 1024
DEPTH = 2
S5_WIDTH = 512
S5_GROUP = 16
S5_GROUPS = 32
S5_STATE = 64
S5_SLABS = 4
SSD_INNER = 1024
SSD_HEAD_DIM = 64
SSD_HEADS = 16
SSD_GROUPS = 4
SSD_HPG = 4
SSD_STATE = 128
SSD_CONV = 4
SSD_CONV_DIM = 2048
HG_WIDTH = 512
HG_EXPAND = 128
HG_HEADS = 4
HG_CHUNK = 64
HG_SUB = 16
MIX_CHUNK = 128
FFN_DIM = 2816
FFN_CONV = 3
FFN_COL = 256
LANES = 128
HALO = 8
ROW_TILE = 512
VMEM_LIMIT = 56 * 1024 * 1024


def _cparams(*sem):
    return pltpu.CompilerParams(dimension_semantics=sem, vmem_limit_bytes=VMEM_LIMIT)


def _resident(shape):
    n = len(shape)
    return pl.BlockSpec(shape, lambda *_: (0,) * n, pipeline_mode=pl.Buffered(1))


def _rms(x, w):
    return x * lax.rsqrt(jnp.mean(x * x, axis=-1, keepdims=True) + EPS) * w


def _silu(x):
    return x * jax.nn.sigmoid(x)


def _dot(a, b):
    return jnp.dot(a, b, preferred_element_type=F32)


def _dot_nt(a, b):
    return lax.dot_general(a, b, (((1,), (1,)), ((), ())), preferred_element_type=F32)


def _dot_tn(a, b):
    return lax.dot_general(a, b, (((0,), (0,)), ((), ())), preferred_element_type=F32)


def _cumsum_rows(tri, x):
    return jnp.dot(tri, x, preferred_element_type=F32, precision=lax.Precision.HIGHEST)


def _tri(t):
    r = lax.broadcasted_iota(jnp.int32, (t, t), 0)
    c = lax.broadcasted_iota(jnp.int32, (t, t), 1)
    return r >= c


def _gelu_tanh(x):
    c = 0.7978845608028654
    return 0.5 * x * (1.0 + jnp.tanh(c * (x + 0.044715 * (x * x * x))))


def _s5_kernel(x_ref, nw_ref, wu_ref, h0r_ref, h0i_ref, ar_ref, ai_ref, bbd_ref, cbd_ref, d_ref, gw_ref, gb_ref,
               o_ref, hr_ref, hi_ref, ubm, utm, xs, ytm, st):
    nb, tt, _ = x_ref.shape
    half = S5_STATE * S5_GROUPS // S5_SLABS

    @pl.when(pl.program_id(0) == 0)
    def _():
        for j in range(S5_SLABS):
            st[2 * j] = h0r_ref[:, j * half:(j + 1) * half]
            st[2 * j + 1] = h0i_ref[:, j * half:(j + 1) * half]

    hn = _rms(x_ref[...].reshape(nb * tt, D_MODEL), nw_ref[...]).astype(BF16)
    ubm[...] = _dot(hn, wu_ref[...]).reshape(nb, tt, S5_WIDTH)

    def to_time_major(t, c):
        utm[t] = ubm[:, t, :]
        return c
    lax.fori_loop(0, tt, to_time_major, 0)

    u2 = utm[...].reshape(tt * nb, S5_WIDTH)
    for j in range(S5_SLABS):
        xs[...] = _dot(u2[:, j * LANES:(j + 1) * LANES].astype(BF16), bbd_ref[j])
        ar = jnp.broadcast_to(ar_ref[:, j * half:(j + 1) * half], (nb, half))
        ai = jnp.broadcast_to(ai_ref[:, j * half:(j + 1) * half], (nb, half))

        def scan_step(t, carry):
            sr, si = carry
            rows = pl.ds(pl.multiple_of(t * nb, nb), nb)
            nr = ar * sr - ai * si + xs[rows, 0:half]
            ni = ar * si + ai * sr + xs[rows, half:2 * half]
            xs[rows, 0:half] = nr
            xs[rows, half:2 * half] = ni
            return nr, ni
        sr, si = lax.fori_loop(0, tt, scan_step, (st[2 * j], st[2 * j + 1]))
        st[2 * j] = sr
        st[2 * j + 1] = si
        hr_ref[:, j * half:(j + 1) * half] = sr
        hi_ref[:, j * half:(j + 1) * half] = si
        ytm[:, j * LANES:(j + 1) * LANES] = _dot(xs[...].astype(BF16), cbd_ref[j])

    y = ytm[...] + d_ref[...] * u2
    a = _gelu_tanh(y)
    out = a * jax.nn.sigmoid(_dot(a.astype(BF16), gw_ref[...]) + gb_ref[...])
    utm[...] = out.reshape(tt, nb, S5_WIDTH)
    for b in range(nb):
        o_ref[b] = utm[:, b, :].astype(o_ref.dtype)


def _s5_mixer(x, norm_w, w_u, h0r, h0i, prm):
    b, n, _ = x.shape
    tt = ROW_TILE // b
    nst = S5_GROUPS * S5_STATE
    half = nst // S5_SLABS
    return pl.pallas_call(
        _s5_kernel,
        grid=(n // tt,),
        in_specs=[pl.BlockSpec((b, tt, D_MODEL), lambda t: (0, t, 0)),
                  _resident((1, D_MODEL)), _resident((D_MODEL, S5_WIDTH)),
                  _resident((b, nst)), _resident((b, nst)),
                  _resident((1, nst)), _resident((1, nst)),
                  _resident((S5_SLABS, LANES, 2 * half)), _resident((S5_SLABS, 2 * half, LANES)),
                  _resident((1, S5_WIDTH)), _resident((S5_WIDTH, S5_WIDTH)), _resident((1, S5_WIDTH))],
        out_specs=(pl.BlockSpec((b, tt, S5_WIDTH), lambda t: (0, t, 0)),
                   pl.BlockSpec((b, nst), lambda t: (0, 0)),
                   pl.BlockSpec((b, nst), lambda t: (0, 0))),
        out_shape=(jax.ShapeDtypeStruct((b, n, S5_WIDTH), BF16),
                   jax.ShapeDtypeStruct((b, nst), F32),
                   jax.ShapeDtypeStruct((b, nst), F32)),
        scratch_shapes=[pltpu.VMEM((b, tt, S5_WIDTH), F32),
                        pltpu.VMEM((tt, b, S5_WIDTH), F32),
                        pltpu.VMEM((tt * b, 2 * half), F32),
                        pltpu.VMEM((tt * b, S5_WIDTH), F32),
                        pltpu.VMEM((2 * S5_SLABS, b, half), F32)],
        compiler_params=_cparams("arbitrary"),
        name="s5_mixer",
    )(x, norm_w, w_u, h0r, h0i, prm["ar"], prm["ai"], prm["bbd"], prm["cbd"], prm["d"], prm["glu_w"], prm["glu_b"])


def _s5_params(lam_re, lam_im, log_dt, b_re, b_im, c_re, c_im, d_skip, glu_w, glu_b):
    dt = jnp.exp(log_dt)[:, None]
    mag = jnp.exp(lam_re * dt)
    ar, ai = mag * jnp.cos(lam_im * dt), mag * jnp.sin(lam_im * dt)
    den = lam_re * lam_re + lam_im * lam_im
    nr, ni = ar - 1.0, ai
    cr = (nr * lam_re + ni * lam_im) / den
    ci = (ni * lam_re - nr * lam_im) / den
    bbr = cr[..., None] * b_re - ci[..., None] * b_im
    bbi = cr[..., None] * b_im + ci[..., None] * b_re
    gps = S5_GROUPS // S5_SLABS
    eye = jnp.eye(gps, dtype=F32)

    def pack_b(bb):
        bb = bb.reshape(S5_SLABS, gps, S5_STATE, S5_GROUP)
        return jnp.einsum("sgph,gk->sghkp", bb, eye).reshape(S5_SLABS, gps * S5_GROUP, gps * S5_STATE)

    def pack_c(cc):
        cc = cc.reshape(S5_SLABS, gps, S5_GROUP, S5_STATE)
        return jnp.einsum("sghp,gk->sgpkh", cc, eye).reshape(S5_SLABS, gps * S5_STATE, gps * S5_GROUP)

    bbd = jnp.concatenate([pack_b(bbr), pack_b(bbi)], axis=2).astype(BF16)
    cbd = jnp.concatenate([pack_c(c_re), -pack_c(c_im)], axis=1).astype(BF16)
    return dict(ar=ar.reshape(1, -1), ai=ai.reshape(1, -1), bbd=bbd, cbd=cbd,
                d=d_skip.reshape(1, -1), glu_w=glu_w.astype(BF16), glu_b=glu_b.reshape(1, -1))


def _ssd_chunk(t, z, dt_raw, cw_ref, cb_ref, dtb_ref, alog_ref, dsk_ref, nw_ref, buf_ref, cbuf, ht, ybuf, xwbuf):
    gs = SSD_HPG * SSD_HEAD_DIM
    acc = cb_ref[...] + cw_ref[SSD_CONV - 1:SSD_CONV, :] * cbuf[HALO:HALO + t, :]
    for j in range(SSD_CONV - 1):
        off = HALO - (SSD_CONV - 1) + j
        acc = acc + cw_ref[j:j + 1, :] * cbuf[off:off + t, :]
    tail = cbuf[t:t + HALO, :]
    cbuf[0:HALO, :] = tail
    buf_ref[0] = tail
    xbc = _silu(acc)
    xs = xbc[:, :SSD_INNER]
    bmat = xbc[:, SSD_INNER:SSD_INNER + SSD_GROUPS * SSD_STATE].astype(BF16)
    cmat = xbc[:, SSD_INNER + SSD_GROUPS * SSD_STATE:].astype(BF16)

    dt = jax.nn.softplus(dt_raw + dtb_ref[...])
    la = dt * (-jnp.exp(alog_ref[...]))
    mask = _tri(t)
    cum = _cumsum_rows(mask.astype(F32), la)
    cum_t = cum.T
    last = cum[t - 1:t, :]
    ecum = jnp.exp(cum)
    wgt = jnp.exp(last - cum) * dt
    elast = jnp.exp(last)

    for g in range(SSD_GROUPS):
        bg = bmat[:, g * SSD_STATE:(g + 1) * SSD_STATE]
        cg = cmat[:, g * SSD_STATE:(g + 1) * SSD_STATE]
        scores = _dot_nt(cg, bg)
        hg = ht[:, g * gs:(g + 1) * gs]
        yint = _dot(cg, hg.astype(BF16))
        for r in range(SSD_HPG):
            h = g * SSD_HPG + r
            xh = xs[:, h * SSD_HEAD_DIM:(h + 1) * SSD_HEAD_DIM]
            seg = cum[:, h:h + 1] - cum_t[h:h + 1, :]
            decay = jnp.exp(jnp.where(mask, seg, -jnp.inf))
            m = (scores * decay).astype(BF16)
            y = _dot(m, (dt[:, h:h + 1] * xh).astype(BF16))
            y = y + yint[:, r * SSD_HEAD_DIM:(r + 1) * SSD_HEAD_DIM] * ecum[:, h:h + 1]
            ybuf[:, h * SSD_HEAD_DIM:(h + 1) * SSD_HEAD_DIM] = y
            xwbuf[:, h * SSD_HEAD_DIM:(h + 1) * SSD_HEAD_DIM] = (wgt[:, h:h + 1] * xh).astype(BF16)
            ht[:, h * SSD_HEAD_DIM:(h + 1) * SSD_HEAD_DIM] = (
                hg[:, r * SSD_HEAD_DIM:(r + 1) * SSD_HEAD_DIM] * elast[:, h:h + 1])
        ht[:, g * gs:(g + 1) * gs] += _dot_tn(bg, xwbuf[:, g * gs:(g + 1) * gs])

    y = ybuf[...] + dsk_ref[...] * xs
    return _rms(y * _silu(z), nw_ref[...])


def _pad_lanes(v):
    return jnp.pad(v, (0, LANES - v.shape[0])).reshape(1, LANES)


def _ssd_params(conv_w, conv_b, dt_bias, a_log, d_skip, norm_w):
    return dict(conv_w=conv_w, conv_b=conv_b.reshape(1, -1), dt_bias=_pad_lanes(dt_bias), a_log=_pad_lanes(a_log),
                d=jnp.repeat(d_skip, SSD_HEAD_DIM).reshape(1, -1), norm_w=norm_w.reshape(1, -1))


def _hgrn_chunk(q, fz, iv, g, lb, nw, st, att_s):
    t = q.shape[0]
    nsub = t // HG_SUB
    k = HG_EXPAND
    lsp = jnp.minimum(fz, 0.0) - jnp.log1p(jnp.exp(-jnp.abs(fz)))
    lsn = lsp - fz
    lbt = jnp.log(lb) + lsn
    log_f = jnp.maximum(lsp, lbt) + jnp.log1p(jnp.exp(-jnp.abs(lsp - lbt)))
    kk = (1.0 - lb) * jax.nn.sigmoid(-fz)

    cum = _cumsum_rows(_tri(t).astype(F32), log_f)
    last = cum[t - 1:t, :]
    qe = (q * jnp.exp(cum)).astype(BF16)
    ke = (kk * jnp.exp(last - cum)).astype(BF16)
    elast = jnp.exp(last)

    qloc, kloc, refs = [], [], []
    for m in range(nsub):
        rows = slice(m * HG_SUB, (m + 1) * HG_SUB)
        r_in = cum[m * HG_SUB - 1:m * HG_SUB, :] if m else None
        r_out = cum[(m + 1) * HG_SUB - 1:(m + 1) * HG_SUB, :]
        qloc.append(q[rows] * jnp.exp(cum[rows] - r_in) if m else None)
        kloc.append((kk[rows] * jnp.exp(r_out - cum[rows])).astype(BF16))
        refs.append((r_in, r_out))
    kl = jnp.concatenate(kloc, axis=0)
    pairs = [(i, j) for i in range(1, nsub) for j in range(i)]
    if pairs:
        lhs = jnp.concatenate(
            [(qloc[i] * jnp.exp(refs[i][0] - refs[j][1])).astype(BF16) for i, j in pairs], axis=0)
    col = lax.broadcasted_iota(jnp.int32, (HG_SUB, t), 1)
    colblk = col // HG_SUB
    row1 = lax.broadcasted_iota(jnp.int32, (HG_SUB, 1), 0)

    for h in range(HG_HEADS):
        hs = slice(h * k, (h + 1) * k)
        if pairs:
            blk = _dot_nt(lhs[:, hs], kl[:, hs])
        for i in range(nsub):
            acc = jnp.zeros((HG_SUB, t), F32)
            for p, (pi, pj) in enumerate(pairs):
                if pi == i:
                    acc = jnp.where(colblk == pj, blk[p * HG_SUB:(p + 1) * HG_SUB, :], acc)
            att_s[h, i * HG_SUB:(i + 1) * HG_SUB, :] = acc

    for i in range(nsub):
        rows = slice(i * HG_SUB, (i + 1) * HG_SUB)
        qi = q[rows]
        ci = cum[rows]
        accs = [att_s[h, rows, :] for h in range(HG_HEADS)]
        for s in range(HG_SUB):
            gs = i * HG_SUB + s
            p = qi * jnp.exp(jnp.minimum(ci - cum[gs:gs + 1, :], 0.0)) * kk[gs:gs + 1, :]
            for h in range(HG_HEADS):
                a = jnp.sum(p[:, h * k:(h + 1) * k], axis=-1, keepdims=True)
                accs[h] = jnp.where(col == gs, jnp.where(row1 >= s, a, 0.0), accs[h])
        for h in range(HG_HEADS):
            att_s[h, rows, :] = accs[h]

    outs = []
    for h in range(HG_HEADS):
        hs = slice(h * k, (h + 1) * k)
        sh = st[h]
        o = _dot(att_s[h].astype(BF16), iv[:, hs]) + _dot_nt(qe[:, hs], sh.astype(BF16))
        st[h] = sh * elast[:, hs] + _dot_tn(iv[:, hs], ke[:, hs])
        outs.append(_rms(o, nw) * _silu(g[:, hs]))
    return outs


_MIX_COLS = {}
_off = 0
for _name, _width in (("z", SSD_INNER), ("xbc", SSD_CONV_DIM), ("q", HG_WIDTH), ("f", HG_WIDTH), ("i", HG_WIDTH),
                      ("g", HG_WIDTH), ("gates", 3 * D_MODEL), ("dt", LANES)):
    _MIX_COLS[_name] = (_off, _width)
    _off += _width
_MIX_WIDTH = _off


def _mix_kernel(x_ref, nw_ref, w_ref, h0_ref, cache_ref, s0_ref,
                cw_ref, cb_ref, dtb_ref, alog_ref, dsk_ref, snw_ref, lb_ref, hnw_ref,
                yb_ref, yc_ref, gates_ref, hn_ref, buf_ref, sn_ref,
                cbuf, ht, ybuf, xwbuf, st, att_s):
    t = x_ref.shape[1]
    c = pl.program_id(1)
    gs = SSD_HPG * SSD_HEAD_DIM
    k = HG_EXPAND

    @pl.when(c == 0)
    def _():
        cbuf[0:HALO, :] = cache_ref[0]
        for g in range(SSD_GROUPS):
            ht[:, g * gs:(g + 1) * gs] = h0_ref[0, g * gs:(g + 1) * gs, :].T
        for h in range(HG_HEADS):
            st[h] = s0_ref[0, h * k:(h + 1) * k, :].T

    hn = _rms(x_ref[0], nw_ref[...]).astype(BF16)

    def proj(name, lo=0, width=None):
        off, full = _MIX_COLS[name]
        width = full if width is None else width
        return _dot(hn, w_ref[:, off + lo:off + lo + width])

    for lo in range(0, 3 * D_MODEL, D_MODEL):
        gates_ref[0, :, lo:lo + D_MODEL] = proj("gates", lo, D_MODEL).astype(gates_ref.dtype)
    for lo in range(0, SSD_CONV_DIM, 512):
        cbuf[HALO:HALO + t, lo:lo + 512] = proj("xbc", lo, 512)
    y = _ssd_chunk(t, proj("z"), proj("dt"), cw_ref, cb_ref, dtb_ref, alog_ref, dsk_ref, snw_ref,
                   buf_ref, cbuf, ht, ybuf, xwbuf)
    yb_ref[0] = y.astype(yb_ref.dtype)

    q, fz, iv, g = proj("q"), proj("f"), proj("i").astype(BF16), proj("g")
    tc = min(t, HG_CHUNK)
    for r0 in range(0, t, tc):
        rows = slice(r0, r0 + tc)
        outs = _hgrn_chunk(q[rows], fz[rows], iv[rows], g[rows], lb_ref[...], hnw_ref[...], st, att_s)
        for h in range(HG_HEADS):
            yc_ref[0, rows, h * k:(h + 1) * k] = outs[h].astype(yc_ref.dtype)

    @pl.when(c == pl.num_programs(1) - 1)
    def _():
        for g in range(SSD_GROUPS):
            hn_ref[0, g * gs:(g + 1) * gs, :] = ht[:, g * gs:(g + 1) * gs].T
        for h in range(HG_HEADS):
            sn_ref[0, h * k:(h + 1) * k, :] = st[h].T


def _mix(x, norm_w, w_mix, ssd_h0, ssd_cache, hg_s0, ssd, hg_lb, hg_norm_w):
    b, n, _ = x.shape
    t = min(n, MIX_CHUNK)
    tc = min(t, HG_CHUNK)
    srows = SSD_HEADS * SSD_HEAD_DIM
    hrows = HG_HEADS * HG_EXPAND
    tok = lambda w: pl.BlockSpec((1, t, w), lambda i, c: (i, c, 0))
    per_seq = lambda r, w: pl.BlockSpec((1, r, w), lambda i, c: (i, 0, 0))
    return pl.pallas_call(
        _mix_kernel,
        grid=(b, n // t),
        in_specs=[tok(D_MODEL), _resident((1, D_MODEL)), _resident((D_MODEL, _MIX_WIDTH)),
                  per_seq(srows, SSD_STATE), per_seq(HALO, SSD_CONV_DIM), per_seq(hrows, HG_EXPAND),
                  _resident((SSD_CONV, SSD_CONV_DIM)), _resident((1, SSD_CONV_DIM)),
                  _resident((1, LANES)), _resident((1, LANES)),
                  _resident((1, SSD_INNER)), _resident((1, SSD_INNER)),
                  _resident((1, HG_WIDTH)), _resident((1, HG_EXPAND))],
        out_specs=(tok(SSD_INNER), tok(HG_WIDTH), tok(3 * D_MODEL),
                   per_seq(srows, SSD_STATE), per_seq(HALO, SSD_CONV_DIM), per_seq(hrows, HG_EXPAND)),
        out_shape=(jax.ShapeDtypeStruct((b, n, SSD_INNER), BF16),
                   jax.ShapeDtypeStruct((b, n, HG_WIDTH), BF16),
                   jax.ShapeDtypeStruct((b, n, 3 * D_MODEL), BF16),
                   jax.ShapeDtypeStruct((b, srows, SSD_STATE), F32),
                   jax.ShapeDtypeStruct((b, HALO, SSD_CONV_DIM), F32),
                   jax.ShapeDtypeStruct((b, hrows, HG_EXPAND), F32)),
        scratch_shapes=[pltpu.VMEM((t + HALO, SSD_CONV_DIM), F32),
                        pltpu.VMEM((SSD_STATE, srows), F32),
                        pltpu.VMEM((t, SSD_INNER), F32),
                        pltpu.VMEM((t, SSD_INNER), BF16),
                        pltpu.VMEM((HG_HEADS, HG_EXPAND, HG_EXPAND), F32),
                        pltpu.VMEM((HG_HEADS, tc, tc), F32)],
        compiler_params=_cparams("parallel", "arbitrary"),
        name="mix",
    )(x, norm_w, w_mix, ssd_h0, ssd_cache, hg_s0, ssd["conv_w"], ssd["conv_b"], ssd["dt_bias"], ssd["a_log"],
      ssd["d"], ssd["norm_w"], hg_lb, hg_norm_w)


def _merge_kernel(x_ref, ya_ref, yb_ref, yc_ref, gl_ref, wa_ref, wb_ref, wc_ref, wo_ref, o_ref):
    d = D_MODEL
    merged = jax.nn.sigmoid(gl_ref[:, 0:d].astype(F32)) * _dot(ya_ref[...], wa_ref[...])
    merged = merged + jax.nn.sigmoid(gl_ref[:, d:2 * d].astype(F32)) * _dot(yb_ref[...], wb_ref[...])
    merged = merged + jax.nn.sigmoid(gl_ref[:, 2 * d:3 * d].astype(F32)) * _dot(yc_ref[...], wc_ref[...])
    o_ref[...] = x_ref[...] + _dot(merged.astype(BF16), wo_ref[...])


def _merge(x2d, ya, yb, yc, gl, wa, wb, wc, wo):
    m = x2d.shape[0]
    tm = min(m, ROW_TILE)
    row = lambda w: pl.BlockSpec((tm, w), lambda i: (i, 0))
    return pl.pallas_call(
        _merge_kernel,
        grid=(m // tm,),
        in_specs=[row(D_MODEL), row(S5_WIDTH), row(SSD_INNER), row(HG_WIDTH), row(3 * D_MODEL),
                  _resident(wa.shape), _resident(wb.shape), _resident(wc.shape), _resident(wo.shape)],
        out_specs=row(D_MODEL),
        out_shape=jax.ShapeDtypeStruct((m, D_MODEL), F32),
        compiler_params=_cparams("parallel"),
        name="merge",
    )(x2d, ya, yb, yc, gl, wa, wb, wc, wo)


def _ffn_kernel(x_ref, cache_ref, nw_ref, wup_ref, cw_ref, cb_ref, wdn_ref, fw_ref,
                o_ref, buf_ref, abuf, gbuf, *, final_norm):
    nb, tt, _ = x_ref.shape

    @pl.when(pl.program_id(1) == 0)
    def _():
        abuf[:, 0:HALO, :] = cache_ref[...]

    x = x_ref[...].reshape(nb * tt, D_MODEL)
    hn = _rms(x, nw_ref[...]).astype(BF16)
    for c0 in range(0, FFN_DIM, FFN_COL):
        cs = slice(c0, c0 + FFN_COL)
        a = _dot(hn, wup_ref[:, cs])
        v = _dot(hn, wup_ref[:, FFN_DIM + c0:FFN_DIM + c0 + FFN_COL])
        abuf[:, HALO:HALO + tt, cs] = a.reshape(nb, tt, FFN_COL)
        conv = cb_ref[:, cs] + cw_ref[FFN_CONV - 1:FFN_CONV, cs] * a.reshape(nb, tt, FFN_COL)
        for j in range(FFN_CONV - 1):
            off = HALO - (FFN_CONV - 1) + j
            conv = conv + cw_ref[j:j + 1, cs] * abuf[:, off:off + tt, cs]
        gbuf[:, cs] = (_silu(conv).reshape(nb * tt, FFN_COL) * v).astype(BF16)
    tail = abuf[:, tt:tt + HALO, :]
    abuf[:, 0:HALO, :] = tail
    buf_ref[...] = tail
    y = x + _dot(gbuf[...], wdn_ref[...])
    if final_norm:
        y = _rms(y, fw_ref[...])
    o_ref[...] = y.reshape(nb, tt, D_MODEL)


def _ffn(x, cache, norm_w, w_up, conv_w, conv_b, w_down, final_w, final_norm):
    b, n, _ = x.shape
    tt = min(n, ROW_TILE)
    nb = ROW_TILE // tt
    return pl.pallas_call(
        functools.partial(_ffn_kernel, final_norm=final_norm),
        grid=(b // nb, n // tt),
        in_specs=[pl.BlockSpec((nb, tt, D_MODEL), lambda i, c: (i, c, 0)),
                  pl.BlockSpec((nb, HALO, FFN_DIM), lambda i, c: (i, 0, 0)),
                  _resident((1, D_MODEL)), _resident((D_MODEL, 2 * FFN_DIM)),
                  _resident((FFN_CONV, FFN_DIM)), _resident((1, FFN_DIM)),
                  _resident((FFN_DIM, D_MODEL)), _resident((1, D_MODEL))],
        out_specs=(pl.BlockSpec((nb, tt, D_MODEL), lambda i, c: (i, c, 0)),
                   pl.BlockSpec((nb, HALO, FFN_DIM), lambda i, c: (i, 0, 0))),
        out_shape=(jax.ShapeDtypeStruct((b, n, D_MODEL), F32),
                   jax.ShapeDtypeStruct((b, HALO, FFN_DIM), F32)),
        scratch_shapes=[pltpu.VMEM((nb, tt + HALO, FFN_DIM), F32),
                        pltpu.VMEM((nb * tt, FFN_DIM), BF16)],
        compiler_params=_cparams("parallel", "arbitrary"),
        name="ffn",
    )(x, cache, norm_w, w_up, conv_w, conv_b, w_down, final_w)


def _pad_cache(cache):
    return jnp.pad(cache, ((0, 0), (HALO - cache.shape[1], 0), (0, 0)))


def _split_w_in(w_in):
    sizes = (S5_WIDTH, SSD_INNER, SSD_CONV_DIM, SSD_HEADS, HG_WIDTH, HG_WIDTH, HG_WIDTH, HG_WIDTH, 3 * D_MODEL)
    offs = [0]
    for s in sizes:
        offs.append(offs[-1] + s)
    u, z, xbc, dt, q, f, i, g, gates = (w_in[:, offs[k]:offs[k + 1]] for k in range(len(sizes)))
    dt = jnp.pad(dt, ((0, 0), (0, LANES - SSD_HEADS)))
    return u.astype(BF16), jnp.concatenate([z, xbc, q, f, i, g, gates, dt], axis=1).astype(BF16)


def _run_trunk(x, s5_re, s5_im, ssd_h, ssd_buf, hg_s, ffn_buf, layers, final_w):
    b, n, _ = x.shape
    m = b * n
    outs = ([], [], [], [], [], [])
    for l, w in enumerate(layers):
        yb, yc, gates, hb, bufb, sc = _mix(x, w["norm_mix_w"], w["w_mix"], ssd_h[l].reshape(b, -1, SSD_STATE),
                                           _pad_cache(ssd_buf[l]), hg_s[l].reshape(b, -1, HG_EXPAND),
                                           w["ssd"], w["hg_lb"], w["hg_norm_w"])
        ya, hr, hi = _s5_mixer(x, w["norm_mix_w"], w["w_u"], s5_re[l].reshape(b, -1), s5_im[l].reshape(b, -1), w["s5"])
        h2d = _merge(x.reshape(m, D_MODEL), ya.reshape(m, -1), yb.reshape(m, -1), yc.reshape(m, -1),
                     gates.reshape(m, -1), w["w_branch_a"], w["w_branch_b"], w["w_branch_c"], w["w_out"])
        x, fb = _ffn(h2d.reshape(b, n, D_MODEL), _pad_cache(ffn_buf[l]), w["norm_ffn_w"], w["ffn_w_up"],
                     w["ffn_conv_w"], w["ffn_conv_b"], w["ffn_w_down"], final_w, l == len(layers) - 1)
        vals = (hr.reshape(b, S5_GROUPS, S5_STATE), hi.reshape(b, S5_GROUPS, S5_STATE),
                hb.reshape(b, SSD_HEADS, SSD_HEAD_DIM, SSD_STATE), bufb[:, HALO - (SSD_CONV - 1):],
                sc.reshape(b, HG_HEADS, HG_EXPAND, HG_EXPAND), fb[:, HALO - (FFN_CONV - 1):])
        for lst, val in zip(outs, vals):
            lst.append(val)
    return (x,) + tuple(jnp.stack(v) for v in outs)


def kernel(x_prompt, x_sample, state_s5_re, state_s5_im, state_ssd, cache_ssd_conv, state_hgrn, cache_ffn_conv,
           norm_mix_w, w_in, s5_lambda_re, s5_lambda_im, s5_log_dt, s5_b_re, s5_b_im, s5_c_re, s5_c_im,
           s5_d, s5_glu_w, s5_glu_b, ssd_conv_w, ssd_conv_b, ssd_dt_bias, ssd_a_log, ssd_d, ssd_norm_w,
           hg_lb_logits, hg_norm_w, w_branch_a, w_branch_b, w_branch_c, w_out,
           norm_ffn_w, ffn_w_up, ffn_conv_w, ffn_conv_b, ffn_w_down, norm_final_w):
    lb_cum = jnp.cumsum(jax.nn.softmax(hg_lb_logits.astype(F32), axis=0), axis=0)
    hg_lb = lb_cum - lb_cum[0]
    layers = []
    for l in range(DEPTH):
        w_u, w_mix = _split_w_in(w_in[l])
        layers.append(dict(
            norm_mix_w=norm_mix_w[l].reshape(1, -1), w_u=w_u, w_mix=w_mix,
            s5=_s5_params(s5_lambda_re[l], s5_lambda_im[l], s5_log_dt[l], s5_b_re[l], s5_b_im[l],
                          s5_c_re[l], s5_c_im[l], s5_d[l], s5_glu_w[l], s5_glu_b[l]),
            ssd=_ssd_params(ssd_conv_w[l], ssd_conv_b[l], ssd_dt_bias[l], ssd_a_log[l], ssd_d[l], ssd_norm_w[l]),
            hg_lb=hg_lb[l].reshape(1, -1), hg_norm_w=hg_norm_w[l].reshape(1, -1),
            w_branch_a=w_branch_a[l].astype(BF16), w_branch_b=w_branch_b[l].astype(BF16),
            w_branch_c=w_branch_c[l].astype(BF16), w_out=w_out[l].astype(BF16),
            norm_ffn_w=norm_ffn_w[l].reshape(1, -1), ffn_w_up=ffn_w_up[l].astype(BF16),
            ffn_conv_w=ffn_conv_w[l], ffn_conv_b=ffn_conv_b[l].reshape(1, -1),
            ffn_w_down=ffn_w_down[l].astype(BF16)))
    final_w = norm_final_w.reshape(1, -1)
    bp = x_prompt.shape[0]
    zeros = lambda *s: jnp.zeros((DEPTH, bp) + s, F32)
    p = _run_trunk(x_prompt, zeros(S5_GROUPS, S5_STATE), zeros(S5_GROUPS, S5_STATE),
                   zeros(SSD_HEADS, SSD_HEAD_DIM, SSD_STATE), zeros(SSD_CONV - 1, SSD_CONV_DIM),
                   zeros(HG_HEADS, HG_EXPAND, HG_EXPAND), zeros(FFN_CONV - 1, FFN_DIM), layers, final_w)
    s = _run_trunk(x_sample, state_s5_re, state_s5_im, state_ssd, cache_ssd_conv, state_hgrn, cache_ffn_conv,
                   layers, final_w)
    return (p[0], s[0]) + p[1:] + s[1:]
```

```python
import functools

import jax
import jax.numpy as jnp
from jax import lax
from jax.experimental import pallas as pl
from jax.experimental.pallas import tpu as pltpu

F32 = jnp.float32
BF16 = jnp.bfloat16
EPS = 1e-6

D_MODEL = 1024
DEPTH = 2
S5_WIDTH = 512
S5_GROUP = 16
S5_GROUPS = 32
S5_STATE = 64
S5_SLABS = 4
SSD_INNER = 1024
SSD_HEAD_DIM = 64
SSD_HEADS = 16
SSD_GROUPS = 4
SSD_HPG = 4
SSD_STATE = 128
SSD_CONV = 4
SSD_CONV_DIM = 2048
HG_WIDTH = 512
HG_EXPAND = 128
HG_HEADS = 4
HG_CHUNK = 64
HG_SUB = 8
LOG2E = 1.4426950408889634
MIX_CHUNK = 128
MIX_ROWS = 256
PROJ_COL = 256
PIECE_COST = PROJ_COL * 42 // 64
CONV_COST = PROJ_COL * 25 // 32
FFN_DIM = 2816
FFN_CONV = 3
FFN_COL = 256
LANES = 128
HALO = 8
ROW_TILE = 512
VMEM_LIMIT = 56 * 1024 * 1024


def _cparams(*sem):
    return pltpu.CompilerParams(dimension_semantics=sem, vmem_limit_bytes=VMEM_LIMIT)


def _resident(shape):
    n = len(shape)
    return pl.BlockSpec(shape, lambda *_: (0,) * n, pipeline_mode=pl.Buffered(1))


def _rms(x, w):
    return x * lax.rsqrt(jnp.mean(x * x, axis=-1, keepdims=True) + EPS) * w


def _silu(x):
    return x * jax.nn.sigmoid(x)


def _dot(a, b):
    return jnp.dot(a, b, preferred_element_type=F32)


def _dot_nt(a, b):
    return lax.dot_general(a, b, (((1,), (1,)), ((), ())), preferred_element_type=F32)


def _dot_tn(a, b):
    return lax.dot_general(a, b, (((0,), (0,)), ((), ())), preferred_element_type=F32)


def _cumsum_rows(tri, x):
    return jnp.dot(tri, x, preferred_element_type=F32, precision=lax.Precision.HIGHEST)


def _tri(t):
    r = lax.broadcasted_iota(jnp.int32, (t, t), 0)
    c = lax.broadcasted_iota(jnp.int32, (t, t), 1)
    return r >= c


def _gelu_tanh(x):
    c = 0.7978845608028654
    return 0.5 * x * (1.0 + jnp.tanh(c * (x + 0.044715 * (x * x * x))))


def _s5_kernel(x_ref, nw_ref, wu_ref, h0r_ref, h0i_ref, ar_ref, ai_ref, bbd_ref, cbd_ref, d_ref, gw_ref, gb_ref,
               o_ref, hr_ref, hi_ref, ubm, utm, xs, ytm, st):
    nb, tt, _ = x_ref.shape
    half = S5_STATE * S5_GROUPS // S5_SLABS

    @pl.when(pl.program_id(0) == 0)
    def _():
        for j in range(S5_SLABS):
            st[2 * j] = h0r_ref[:, j * half:(j + 1) * half]
            st[2 * j + 1] = h0i_ref[:, j * half:(j + 1) * half]

    hn = _rms(x_ref[...].reshape(nb * tt, D_MODEL), nw_ref[...]).astype(BF16)
    ubm[...] = _dot(hn, wu_ref[...]).reshape(nb, tt, S5_WIDTH)

    def to_time_major(t, c):
        utm[t] = ubm[:, t, :]
        return c
    lax.fori_loop(0, tt, to_time_major, 0)

    u2 = utm[...].reshape(tt * nb, S5_WIDTH)
    for j in range(S5_SLABS):
        xs[...] = _dot(u2[:, j * LANES:(j + 1) * LANES].astype(BF16), bbd_ref[j])
        ar = jnp.broadcast_to(ar_ref[:, j * half:(j + 1) * half], (nb, half))
        ai = jnp.broadcast_to(ai_ref[:, j * half:(j + 1) * half], (nb, half))

        def scan_step(t, carry):
            sr, si = carry
            rows = pl.ds(pl.multiple_of(t * nb, nb), nb)
            nr = ar * sr - ai * si + xs[rows, 0:half]
            ni = ar * si + ai * sr + xs[rows, half:2 * half]
            xs[rows, 0:half] = nr
            xs[rows, half:2 * half] = ni
            return nr, ni
        sr, si = lax.fori_loop(0, tt, scan_step, (st[2 * j], st[2 * j + 1]))
        st[2 * j] = sr
        st[2 * j + 1] = si
        hr_ref[:, j * half:(j + 1) * half] = sr
        hi_ref[:, j * half:(j + 1) * half] = si
        ytm[:, j * LANES:(j + 1) * LANES] = _dot(xs[...].astype(BF16), cbd_ref[j])

    y = ytm[...] + d_ref[...] * u2
    a = _gelu_tanh(y)
    out = a * jax.nn.sigmoid(_dot(a.astype(BF16), gw_ref[...]) + gb_ref[...])
    utm[...] = out.reshape(tt, nb, S5_WIDTH)
    for b in range(nb):
        o_ref[b] = utm[:, b, :].astype(o_ref.dtype)


def _s5_mixer(x, norm_w, w_u, h0r, h0i, prm):
    b, n, _ = x.shape
    tt = ROW_TILE // b
    nst = S5_GROUPS * S5_STATE
    half = nst // S5_SLABS
    return pl.pallas_call(
        _s5_kernel,
        grid=(n // tt,),
        in_specs=[pl.BlockSpec((b, tt, D_MODEL), lambda t: (0, t, 0)),
                  _resident((1, D_MODEL)), _resident((D_MODEL, S5_WIDTH)),
                  _resident((b, nst)), _resident((b, nst)),
                  _resident((1, nst)), _resident((1, nst)),
                  _resident((S5_SLABS, LANES, 2 * half)), _resident((S5_SLABS, 2 * half, LANES)),
                  _resident((1, S5_WIDTH)), _resident((S5_WIDTH, S5_WIDTH)), _resident((1, S5_WIDTH))],
        out_specs=(pl.BlockSpec((b, tt, S5_WIDTH), lambda t: (0, t, 0)),
                   pl.BlockSpec((b, nst), lambda t: (0, 0)),
                   pl.BlockSpec((b, nst), lambda t: (0, 0))),
        out_shape=(jax.ShapeDtypeStruct((b, n, S5_WIDTH), BF16),
                   jax.ShapeDtypeStruct((b, nst), F32),
                   jax.ShapeDtypeStruct((b, nst), F32)),
        scratch_shapes=[pltpu.VMEM((b, tt, S5_WIDTH), F32),
                        pltpu.VMEM((tt, b, S5_WIDTH), F32),
                        pltpu.VMEM((tt * b, 2 * half), F32),
                        pltpu.VMEM((tt * b, S5_WIDTH), F32),
                        pltpu.VMEM((2 * S5_SLABS, b, half), F32)],
        compiler_params=_cparams("arbitrary"),
        name="s5_mixer",
    )(x, norm_w, w_u, h0r, h0i, prm["ar"], prm["ai"], prm["bbd"], prm["cbd"], prm["d"], prm["glu_w"], prm["glu_b"])


def _s5_params(lam_re, lam_im, log_dt, b_re, b_im, c_re, c_im, d_skip, glu_w, glu_b):
    dt = jnp.exp(log_dt)[:, None]
    mag = jnp.exp(lam_re * dt)
    ar, ai = mag * jnp.cos(lam_im * dt), mag * jnp.sin(lam_im * dt)
    den = lam_re * lam_re + lam_im * lam_im
    nr, ni = ar - 1.0, ai
    cr = (nr * lam_re + ni * lam_im) / den
    ci = (ni * lam_re - nr * lam_im) / den
    bbr = cr[..., None] * b_re - ci[..., None] * b_im
    bbi = cr[..., None] * b_im + ci[..., None] * b_re
    gps = S5_GROUPS // S5_SLABS
    eye = jnp.eye(gps, dtype=F32)

    def pack_b(bb):
        bb = bb.reshape(S5_SLABS, gps, S5_STATE, S5_GROUP)
        return jnp.einsum("sgph,gk->sghkp", bb, eye).reshape(S5_SLABS, gps * S5_GROUP, gps * S5_STATE)

    def pack_c(cc):
        cc = cc.reshape(S5_SLABS, gps, S5_GROUP, S5_STATE)
        return jnp.einsum("sghp,gk->sgpkh", cc, eye).reshape(S5_SLABS, gps * S5_STATE, gps * S5_GROUP)

    bbd = jnp.concatenate([pack_b(bbr), pack_b(bbi)], axis=2).astype(BF16)
    cbd = jnp.concatenate([pack_c(c_re), -pack_c(c_im)], axis=1).astype(BF16)
    return dict(ar=ar.reshape(1, -1), ai=ai.reshape(1, -1), bbd=bbd, cbd=cbd,
                d=d_skip.reshape(1, -1), glu_w=glu_w.astype(BF16), glu_b=glu_b.reshape(1, -1))


def _ssd_conv_stages(rows, cw_ref, cb_ref, buf_ref, cbuf, xact):
    for c0 in range(0, SSD_CONV_DIM, PROJ_COL):
        cs = slice(c0, c0 + PROJ_COL)
        acc = cb_ref[:, cs] + cw_ref[SSD_CONV - 1:SSD_CONV, cs] * cbuf[HALO:HALO + rows, cs]
        for j in range(SSD_CONV - 1):
            off = HALO - (SSD_CONV - 1) + j
            acc = acc + cw_ref[j:j + 1, cs] * cbuf[off:off + rows, cs]
        xact[:, cs] = _silu(acc)
        tail = cbuf[rows:rows + HALO, cs]
        cbuf[0:HALO, cs] = tail
        buf_ref[0, :, cs] = tail
        yield CONV_COST * rows // MIX_CHUNK


def _ssd_stages(rows, z_ref, dt_ref, dtb_ref, alog_ref, dsk_ref, nw_ref, yb_ref, xact_ref, ht, ybuf, xwbuf):
    t = rows.stop - rows.start
    gs = SSD_HPG * SSD_HEAD_DIM
    xact = xact_ref.at[rows]
    dt = jax.nn.softplus(dt_ref[rows, :] + dtb_ref[...])
    la = dt * (-LOG2E * jnp.exp(alog_ref[...]))
    mask = _tri(t)
    cum = _cumsum_rows(mask.astype(F32), la)
    cum_t = cum.T
    last = cum[t - 1:t, :]
    ecum = jnp.exp2(cum)
    wgt = jnp.exp2(last - cum) * dt
    elast = jnp.exp2(last)
    yield 150

    b0 = SSD_INNER
    c0 = SSD_INNER + SSD_GROUPS * SSD_STATE
    for g in range(SSD_GROUPS):
        bg = xact[:, b0 + g * SSD_STATE:b0 + (g + 1) * SSD_STATE].astype(BF16)
        cg = xact[:, c0 + g * SSD_STATE:c0 + (g + 1) * SSD_STATE].astype(BF16)
        scores = _dot_nt(cg, bg)
        hg = ht[:, g * gs:(g + 1) * gs]
        yint = _dot(cg, hg.astype(BF16))
        for r in range(SSD_HPG):
            h = g * SSD_HPG + r
            hs = slice(h * SSD_HEAD_DIM, (h + 1) * SSD_HEAD_DIM)
            xh = xact[:, hs]
            seg = cum[:, h:h + 1] - cum_t[h:h + 1, :]
            decay = jnp.exp2(jnp.where(mask, seg, -jnp.inf))
            m = (scores * decay).astype(BF16)
            y = _dot(m, (dt[:, h:h + 1] * xh).astype(BF16))
            y = y + yint[:, r * SSD_HEAD_DIM:(r + 1) * SSD_HEAD_DIM] * ecum[:, h:h + 1]
            ybuf[:, hs] = y
            xwbuf[:, hs] = (wgt[:, h:h + 1] * xh).astype(BF16)
            ht[:, hs] = hg[:, r * SSD_HEAD_DIM:(r + 1) * SSD_HEAD_DIM] * elast[:, h:h + 1]
        ht[:, g * gs:(g + 1) * gs] += _dot_tn(bg, xwbuf[:, g * gs:(g + 1) * gs])
        yield 350

    y = ybuf[...] + dsk_ref[...] * xact[:, :SSD_INNER]
    yb_ref[0, rows, :] = _rms(y * _silu(z_ref[rows, :]), nw_ref[...]).astype(yb_ref.dtype)
    yield 300


def _pad_lanes(v):
    return jnp.pad(v, (0, LANES - v.shape[0])).reshape(1, LANES)


def _ssd_params(conv_w, conv_b, dt_bias, a_log, d_skip, norm_w):
    return dict(conv_w=conv_w, conv_b=conv_b.reshape(1, -1), dt_bias=_pad_lanes(dt_bias), a_log=_pad_lanes(a_log),
                d=jnp.repeat(d_skip, SSD_HEAD_DIM).reshape(1, -1), norm_w=norm_w.reshape(1, -1))


def _hgrn_stages(rows, q_ref, f_ref, i_ref, g_ref, lb, nw, yc_ref, st, att_s):
    t = rows.stop - rows.start
    nsub = t // HG_SUB
    k = HG_EXPAND
    q = q_ref[rows, :]
    fz = f_ref[rows, :]
    e = jnp.exp(-jnp.abs(fz))
    lsp = jnp.minimum(fz, 0.0) - jnp.log(1.0 + e)
    lsn = lsp - fz
    lbt = jnp.log(lb) + lsn
    log_f = jnp.maximum(lsp, lbt) + jnp.log(1.0 + jnp.exp(-jnp.abs(lsp - lbt)))
    kk = (1.0 - lb) * (jnp.where(fz >= 0.0, e, 1.0) / (1.0 + e))

    cum = _cumsum_rows(_tri(t).astype(F32), log_f * LOG2E)
    last = cum[t - 1:t, :]
    qe = (q * jnp.exp2(cum)).astype(BF16)
    ke = (kk * jnp.exp2(last - cum)).astype(BF16)
    elast = jnp.exp2(last)
    yield 350

    qloc, kloc, refs = [], [], []
    for m in range(nsub):
        sub = slice(m * HG_SUB, (m + 1) * HG_SUB)
        r_in = cum[m * HG_SUB - 1:m * HG_SUB, :] if m else None
        r_out = cum[(m + 1) * HG_SUB - 1:(m + 1) * HG_SUB, :]
        qloc.append(q[sub] * jnp.exp2(cum[sub] - r_in) if m else None)
        kloc.append(kk[sub] * jnp.exp2(r_out - cum[sub]))
        refs.append((r_in, r_out))
    kl = jnp.concatenate(kloc, axis=0).astype(BF16)
    pairs = [(i, j) for i in range(1, nsub) for j in range(i)]
    if pairs:
        lhs = jnp.concatenate(
            [qloc[i] * jnp.exp2(refs[i][0] - refs[j][1]) for i, j in pairs], axis=0).astype(BF16)
    col = lax.broadcasted_iota(jnp.int32, (HG_SUB, t), 1)
    colblk = col // HG_SUB
    row1 = lax.broadcasted_iota(jnp.int32, (HG_SUB, 1), 0)

    for h in range(HG_HEADS):
        hs = slice(h * k, (h + 1) * k)
        if pairs:
            blk = _dot_nt(lhs[:, hs], kl[:, hs])
        for i in range(nsub):
            acc = jnp.zeros((HG_SUB, t), F32)
            for p, (pi, pj) in enumerate(pairs):
                if pi == i:
                    acc = jnp.where(colblk == pj, blk[p * HG_SUB:(p + 1) * HG_SUB, :], acc)
            att_s[h, i * HG_SUB:(i + 1) * HG_SUB, :] = acc
    yield 150

    for i in range(nsub):
        sub = slice(i * HG_SUB, (i + 1) * HG_SUB)
        qi = q[sub]
        ci = cum[sub]
        accs = [att_s[h, sub, :] for h in range(HG_HEADS)]
        for s in range(HG_SUB):
            gs = i * HG_SUB + s
            p = qi * jnp.exp2(jnp.minimum(ci - cum[gs:gs + 1, :], 0.0)) * kk[gs:gs + 1, :]
            for h in range(HG_HEADS):
                a = jnp.sum(p[:, h * k:(h + 1) * k], axis=-1, keepdims=True)
                accs[h] = jnp.where(col == gs, jnp.where(row1 >= s, a, 0.0), accs[h])
        for h in range(HG_HEADS):
            att_s[h, sub, :] = accs[h]
        if i % 2 == 1:
            yield 80

    iv = i_ref[rows, :].astype(BF16)
    for h in range(HG_HEADS):
        hs = slice(h * k, (h + 1) * k)
        sh = st[h]
        o = _dot(att_s[h].astype(BF16), iv[:, hs]) + _dot_nt(qe[:, hs], sh.astype(BF16))
        st[h] = sh * elast[:, hs] + _dot_tn(iv[:, hs], ke[:, hs])
        yc_ref[0, rows, hs] = (_rms(o, nw) * _silu(g_ref[rows, hs])).astype(yc_ref.dtype)
        if h % 2 == 1:
            yield 150


_MIX_COLS = {}
_off = 0
for _name, _width in (("z", SSD_INNER), ("xbc", SSD_CONV_DIM), ("q", HG_WIDTH), ("f", HG_WIDTH), ("i", HG_WIDTH),
                      ("g", HG_WIDTH), ("gates", 3 * D_MODEL), ("dt", LANES)):
    _MIX_COLS[_name] = (_off, _width)
    _off += _width
_MIX_WIDTH = _off


def _mix_kernel(x_ref, nw_ref, w_ref, h0_ref, cache_ref, s0_ref,
                cw_ref, cb_ref, dtb_ref, alog_ref, dsk_ref, snw_ref, lb_ref, hnw_ref,
                yb_ref, yc_ref, gates_ref, hn_ref, buf_ref, sn_ref,
                cbuf, xact, ht, ybuf, xwbuf, st, att_s, pz, pdt, pq, pf, pi, pg, hnb):
    rows = x_ref.shape[1]
    c = pl.program_id(1)
    gs = SSD_HPG * SSD_HEAD_DIM
    k = HG_EXPAND

    @pl.when(c == 0)
    def _():
        cbuf[0:HALO, :] = cache_ref[0]
        for g in range(SSD_GROUPS):
            ht[:, g * gs:(g + 1) * gs] = h0_ref[0, g * gs:(g + 1) * gs, :].T
        for h in range(HG_HEADS):
            st[h] = s0_ref[0, h * k:(h + 1) * k, :].T

    hnb[...] = _rms(x_ref[0], nw_ref[...]).astype(BF16)

    def piece(name, lo, width, store):
        off, _ = _MIX_COLS[name]
        return lambda: store(_dot(hnb[...], w_ref[:, off + lo:off + lo + width]), lo, width)

    def into(ref, row0=0):
        def store(v, lo, width):
            ref[row0:row0 + rows, lo:lo + width] = v.astype(ref.dtype)
        return store

    def into_gates(v, lo, width):
        gates_ref[0, :, lo:lo + width] = v.astype(gates_ref.dtype)

    pieces = [piece("xbc", lo, PROJ_COL, into(cbuf, HALO)) for lo in range(0, SSD_CONV_DIM, PROJ_COL)]
    pieces.insert(1, piece("dt", 0, LANES, into(pdt)))
    pieces += [piece("z", lo, PROJ_COL, into(pz)) for lo in range(0, SSD_INNER, PROJ_COL)]
    pieces += [piece(name, lo, PROJ_COL, into(ref)) for name, ref in (("q", pq), ("f", pf), ("i", pi), ("g", pg))
               for lo in range(0, HG_WIDTH, PROJ_COL)]
    pieces += [piece("gates", lo, PROJ_COL, into_gates) for lo in range(0, 3 * D_MODEL, PROJ_COL)]

    def issue(n=1):
        for _ in range(min(n, len(pieces))):
            pieces.pop(0)()

    issue(3)
    ts = min(rows, MIX_CHUNK)
    th = min(rows, HG_CHUNK)
    stages = [_ssd_conv_stages(rows, cw_ref, cb_ref, buf_ref, cbuf, xact)]
    stages += [_ssd_stages(slice(r0, r0 + ts), pz, pdt, dtb_ref, alog_ref, dsk_ref, snw_ref, yb_ref,
                           xact, ht, ybuf, xwbuf) for r0 in range(0, rows, ts)]
    stages += [_hgrn_stages(slice(r0, r0 + th), pq, pf, pi, pg, lb_ref[...], hnw_ref[...], yc_ref, st, att_s)
               for r0 in range(0, rows, th)]
    piece_cost = PIECE_COST * rows // MIX_CHUNK
    credit = 0
    for gen in stages:
        for weight in gen:
            credit += weight
            while credit >= piece_cost and pieces:
                issue()
                credit -= piece_cost
    issue(len(pieces))

    @pl.when(c == pl.num_programs(1) - 1)
    def _():
        for g in range(SSD_GROUPS):
            hn_ref[0, g * gs:(g + 1) * gs, :] = ht[:, g * gs:(g + 1) * gs].T
        for h in range(HG_HEADS):
            sn_ref[0, h * k:(h + 1) * k, :] = st[h].T


def _mix(x, norm_w, w_mix, ssd_h0, ssd_cache, hg_s0, ssd, hg_lb, hg_norm_w):
    b, n, _ = x.shape
    t = min(n, MIX_ROWS)
    ts = min(t, MIX_CHUNK)
    th = min(t, HG_CHUNK)
    srows = SSD_HEADS * SSD_HEAD_DIM
    hrows = HG_HEADS * HG_EXPAND
    tok = lambda w: pl.BlockSpec((1, t, w), lambda i, c: (i, c, 0))
    per_seq = lambda r, w: pl.BlockSpec((1, r, w), lambda i, c: (i, 0, 0))
    return pl.pallas_call(
        _mix_kernel,
        grid=(b, n // t),
        in_specs=[tok(D_MODEL), _resident((1, D_MODEL)), _resident((D_MODEL, _MIX_WIDTH)),
                  per_seq(srows, SSD_STATE), per_seq(HALO, SSD_CONV_DIM), per_seq(hrows, HG_EXPAND),
                  _resident((SSD_CONV, SSD_CONV_DIM)), _resident((1, SSD_CONV_DIM)),
                  _resident((1, LANES)), _resident((1, LANES)),
                  _resident((1, SSD_INNER)), _resident((1, SSD_INNER)),
                  _resident((1, HG_WIDTH)), _resident((1, HG_EXPAND))],
        out_specs=(tok(SSD_INNER), tok(HG_WIDTH), tok(3 * D_MODEL),
                   per_seq(srows, SSD_STATE), per_seq(HALO, SSD_CONV_DIM), per_seq(hrows, HG_EXPAND)),
        out_shape=(jax.ShapeDtypeStruct((b, n, SSD_INNER), BF16),
                   jax.ShapeDtypeStruct((b, n, HG_WIDTH), BF16),
                   jax.ShapeDtypeStruct((b, n, 3 * D_MODEL), BF16),
                   jax.ShapeDtypeStruct((b, srows, SSD_STATE), F32),
                   jax.ShapeDtypeStruct((b, HALO, SSD_CONV_DIM), F32),
                   jax.ShapeDtypeStruct((b, hrows, HG_EXPAND), F32)),
        scratch_shapes=[pltpu.VMEM((t + HALO, SSD_CONV_DIM), F32),
                        pltpu.VMEM((t, SSD_CONV_DIM), F32),
                        pltpu.VMEM((SSD_STATE, srows), F32),
                        pltpu.VMEM((ts, SSD_INNER), F32),
                        pltpu.VMEM((ts, SSD_INNER), BF16),
                        pltpu.VMEM((HG_HEADS, HG_EXPAND, HG_EXPAND), F32),
                        pltpu.VMEM((HG_HEADS, th, th), F32),
                        pltpu.VMEM((t, SSD_INNER), F32),
                        pltpu.VMEM((t, LANES), F32),
                        pltpu.VMEM((t, HG_WIDTH), F32), pltpu.VMEM((t, HG_WIDTH), F32),
                        pltpu.VMEM((t, HG_WIDTH), F32), pltpu.VMEM((t, HG_WIDTH), F32),
                        pltpu.VMEM((t, D_MODEL), BF16)],
        compiler_params=_cparams("parallel", "arbitrary"),
        name="mix",
    )(x, norm_w, w_mix, ssd_h0, ssd_cache, hg_s0, ssd["conv_w"], ssd["conv_b"], ssd["dt_bias"], ssd["a_log"],
      ssd["d"], ssd["norm_w"], hg_lb, hg_norm_w)


def _merge_kernel(x_ref, ya_ref, yb_ref, yc_ref, gl_ref, wa_ref, wb_ref, wc_ref, wo_ref, o_ref):
    d = D_MODEL
    merged = jax.nn.sigmoid(gl_ref[:, 0:d].astype(F32)) * _dot(ya_ref[...], wa_ref[...])
    merged = merged + jax.nn.sigmoid(gl_ref[:, d:2 * d].astype(F32)) * _dot(yb_ref[...], wb_ref[...])
    merged = merged + jax.nn.sigmoid(gl_ref[:, 2 * d:3 * d].astype(F32)) * _dot(yc_ref[...], wc_ref[...])
    o_ref[...] = x_ref[...] + _dot(merged.astype(BF16), wo_ref[...])


def _merge(x2d, ya, yb, yc, gl, wa, wb, wc, wo):
    m = x2d.shape[0]
    tm = min(m, ROW_TILE)
    row = lambda w: pl.BlockSpec((tm, w), lambda i: (i, 0))
    return pl.pallas_call(
        _merge_kernel,
        grid=(m // tm,),
        in_specs=[row(D_MODEL), row(S5_WIDTH), row(SSD_INNER), row(HG_WIDTH), row(3 * D_MODEL),
                  _resident(wa.shape), _resident(wb.shape), _resident(wc.shape), _resident(wo.shape)],
        out_specs=row(D_MODEL),
        out_shape=jax.ShapeDtypeStruct((m, D_MODEL), F32),
        compiler_params=_cparams("parallel"),
        name="merge",
    )(x2d, ya, yb, yc, gl, wa, wb, wc, wo)


def _ffn_kernel(x_ref, cache_ref, nw_ref, wup_ref, cw_ref, cb_ref, wdn_ref, fw_ref,
                o_ref, buf_ref, abuf, gbuf, *, final_norm):
    nb, tt, _ = x_ref.shape

    @pl.when(pl.program_id(1) == 0)
    def _():
        abuf[:, 0:HALO, :] = cache_ref[...]

    x = x_ref[...].reshape(nb * tt, D_MODEL)
    hn = _rms(x, nw_ref[...]).astype(BF16)
    for c0 in range(0, FFN_DIM, FFN_COL):
        cs = slice(c0, c0 + FFN_COL)
        a = _dot(hn, wup_ref[:, cs])
        v = _dot(hn, wup_ref[:, FFN_DIM + c0:FFN_DIM + c0 + FFN_COL])
        abuf[:, HALO:HALO + tt, cs] = a.reshape(nb, tt, FFN_COL)
        conv = cb_ref[:, cs] + cw_ref[FFN_CONV - 1:FFN_CONV, cs] * a.reshape(nb, tt, FFN_COL)
        for j in range(FFN_CONV - 1):
            off = HALO - (FFN_CONV - 1) + j
            conv = conv + cw_ref[j:j + 1, cs] * abuf[:, off:off + tt, cs]
        gbuf[:, cs] = (_silu(conv).reshape(nb * tt, FFN_COL) * v).astype(BF16)
    tail = abuf[:, tt:tt + HALO, :]
    abuf[:, 0:HALO, :] = tail
    buf_ref[...] = tail
    y = x + _dot(gbuf[...], wdn_ref[...])
    if final_norm:
        y = _rms(y, fw_ref[...])
    o_ref[...] = y.reshape(nb, tt, D_MODEL)


def _ffn(x, cache, norm_w, w_up, conv_w, conv_b, w_down, final_w, final_norm):
    b, n, _ = x.shape
    tt = min(n, ROW_TILE)
    nb = ROW_TILE // tt
    return pl.pallas_call(
        functools.partial(_ffn_kernel, final_norm=final_norm),
        grid=(b // nb, n // tt),
        in_specs=[pl.BlockSpec((nb, tt, D_MODEL), lambda i, c: (i, c, 0)),
                  pl.BlockSpec((nb, HALO, FFN_DIM), lambda i, c: (i, 0, 0)),
                  _resident((1, D_MODEL)), _resident((D_MODEL, 2 * FFN_DIM)),
                  _resident((FFN_CONV, FFN_DIM)), _resident((1, FFN_DIM)),
                  _resident((FFN_DIM, D_MODEL)), _resident((1, D_MODEL))],
        out_specs=(pl.BlockSpec((nb, tt, D_MODEL), lambda i, c: (i, c, 0)),
                   pl.BlockSpec((nb, HALO, FFN_DIM), lambda i, c: (i, 0, 0))),
        out_shape=(jax.ShapeDtypeStruct((b, n, D_MODEL), F32),
                   jax.ShapeDtypeStruct((b, HALO, FFN_DIM), F32)),
        scratch_shapes=[pltpu.VMEM((nb, tt + HALO, FFN_DIM), F32),
                        pltpu.VMEM((nb * tt, FFN_DIM), BF16)],
        compiler_params=_cparams("parallel", "arbitrary"),
        name="ffn",
    )(x, cache, norm_w, w_up, conv_w, conv_b, w_down, final_w)


def _pad_cache(cache):
    return jnp.pad(cache, ((0, 0), (HALO - cache.shape[1], 0), (0, 0)))


def _split_w_in(w_in):
    sizes = (S5_WIDTH, SSD_INNER, SSD_CONV_DIM, SSD_HEADS, HG_WIDTH, HG_WIDTH, HG_WIDTH, HG_WIDTH, 3 * D_MODEL)
    offs = [0]
    for s in sizes:
        offs.append(offs[-1] + s)
    u, z, xbc, dt, q, f, i, g, gates = (w_in[:, offs[k]:offs[k + 1]] for k in range(len(sizes)))
    dt = jnp.pad(dt, ((0, 0), (0, LANES - SSD_HEADS)))
    return u.astype(BF16), jnp.concatenate([z, xbc, q, f, i, g, gates, dt], axis=1).astype(BF16)


def _run_trunk(x, s5_re, s5_im, ssd_h, ssd_buf, hg_s, ffn_buf, layers, final_w):
    b, n, _ = x.shape
    m = b * n
    outs = ([], [], [], [], [], [])
    for l, w in enumerate(layers):
        yb, yc, gates, hb, bufb, sc = _mix(x, w["norm_mix_w"], w["w_mix"], ssd_h[l].reshape(b, -1, SSD_STATE),
                                           _pad_cache(ssd_buf[l]), hg_s[l].reshape(b, -1, HG_EXPAND),
                                           w["ssd"], w["hg_lb"], w["hg_norm_w"])
        ya, hr, hi = _s5_mixer(x, w["norm_mix_w"], w["w_u"], s5_re[l].reshape(b, -1), s5_im[l].reshape(b, -1), w["s5"])
        h2d = _merge(x.reshape(m, D_MODEL), ya.reshape(m, -1), yb.reshape(m, -1), yc.reshape(m, -1),
                     gates.reshape(m, -1), w["w_branch_a"], w["w_branch_b"], w["w_branch_c"], w["w_out"])
        x, fb = _ffn(h2d.reshape(b, n, D_MODEL), _pad_cache(ffn_buf[l]), w["norm_ffn_w"], w["ffn_w_up"],
                     w["ffn_conv_w"], w["ffn_conv_b"], w["ffn_w_down"], final_w, l == len(layers) - 1)
        vals = (hr.reshape(b, S5_GROUPS, S5_STATE), hi.reshape(b, S5_GROUPS, S5_STATE),
                hb.reshape(b, SSD_HEADS, SSD_HEAD_DIM, SSD_STATE), bufb[:, HALO - (SSD_CONV - 1):],
                sc.reshape(b, HG_HEADS, HG_EXPAND, HG_EXPAND), fb[:, HALO - (FFN_CONV - 1):])
        for lst, val in zip(outs, vals):
            lst.append(val)
    return (x,) + tuple(jnp.stack(v) for v in outs)


def kernel(x_prompt, x_sample, state_s5_re, state_s5_im, state_ssd, cache_ssd_conv, state_hgrn, cache_ffn_conv,
           norm_mix_w, w_in, s5_lambda_re, s5_lambda_im, s5_log_dt, s5_b_re, s5_b_im, s5_c_re, s5_c_im,
           s5_d, s5_glu_w, s5_glu_b, ssd_conv_w, ssd_conv_b, ssd_dt_bias, ssd_a_log, ssd_d, ssd_norm_w,
           hg_lb_logits, hg_norm_w, w_branch_a, w_branch_b, w_branch_c, w_out,
           norm_ffn_w, ffn_w_up, ffn_conv_w, ffn_conv_b, ffn_w_down, norm_final_w):
    lb_cum = jnp.cumsum(jax.nn.softmax(hg_lb_logits.astype(F32), axis=0), axis=0)
    hg_lb = lb_cum - lb_cum[0]
    layers = []
    for l in range(DEPTH):
        w_u, w_mix = _split_w_in(w_in[l])
        layers.append(dict(
            norm_mix_w=norm_mix_w[l].reshape(1, -1), w_u=w_u, w_mix=w_mix,
            s5=_s5_params(s5_lambda_re[l], s5_lambda_im[l], s5_log_dt[l], s5_b_re[l], s5_b_im[l],
                          s5_c_re[l], s5_c_im[l], s5_d[l], s5_glu_w[l], s5_glu_b[l]),
            ssd=_ssd_params(ssd_conv_w[l], ssd_conv_b[l], ssd_dt_bias[l], ssd_a_log[l], ssd_d[l], ssd_norm_w[l]),
            hg_lb=hg_lb[l].reshape(1, -1), hg_norm_w=hg_norm_w[l].reshape(1, -1),
            w_branch_a=w_branch_a[l].astype(BF16), w_branch_b=w_branch_b[l].astype(BF16),
            w_branch_c=w_branch_c[l].astype(BF16), w_out=w_out[l].astype(BF16),
            norm_ffn_w=norm_ffn_w[l].reshape(1, -1), ffn_w_up=ffn_w_up[l].astype(BF16),
            ffn_conv_w=ffn_conv_w[l], ffn_conv_b=ffn_conv_b[l].reshape(1, -1),
            ffn_w_down=ffn_w_down[l].astype(BF16)))
    final_w = norm_final_w.reshape(1, -1)
    bp = x_prompt.shape[0]
    zeros = lambda *s: jnp.zeros((DEPTH, bp) + s, F32)
    p = _run_trunk(x_prompt, zeros(S5_GROUPS, S5_STATE), zeros(S5_GROUPS, S5_STATE),
                   zeros(SSD_HEADS, SSD_HEAD_DIM, SSD_STATE), zeros(SSD_CONV - 1, SSD_CONV_DIM),
                   zeros(HG_HEADS, HG_EXPAND, HG_EXPAND), zeros(FFN_CONV - 1, FFN_DIM), layers, final_w)
    s = _run_trunk(x_sample, state_s5_re, state_s5_im, state_ssd, cache_ssd_conv, state_hgrn, cache_ffn_conv,
                   layers, final_w)
    return (p[0], s[0]) + p[1:] + s[1:]
```

```python
import functools

import jax
import jax.numpy as jnp
from jax import lax
from jax.experimental import pallas as pl
from jax.experimental.pallas import tpu as pltpu

F32 = jnp.float32
BF16 = jnp.bfloat16
EPS = 1e-6

D_MODEL = 1024
DEPTH = 2
S5_WIDTH = 512
S5_GROUP = 16
S5_GROUPS = 32
S5_STATE = 64
S5_SLABS = 4
SSD_INNER = 1024
SSD_HEAD_DIM = 64
SSD_HEADS = 16
SSD_GROUPS = 4
SSD_HPG = 4
SSD_STATE = 128
SSD_CONV = 4
SSD_CONV_DIM = 2048
HG_WIDTH = 512
HG_EXPAND = 128
HG_HEADS = 4
HG_CHUNK = 64
HG_SUB = 8
LOG2E = 1.4426950408889634
MIX_CHUNK = 128
MIX_ROWS = 256
PROJ_COL = 256
PIECE_COST = PROJ_COL * 42 // 64
CONV_COST = PROJ_COL * 25 // 32
FFN_DIM = 2816
FFN_CONV = 3
FFN_COL = 256
LANES = 128
HALO = 8
ROW_TILE = 512
VMEM_LIMIT = 56 * 1024 * 1024


def _cparams(*sem):
    return pltpu.CompilerParams(dimension_semantics=sem, vmem_limit_bytes=VMEM_LIMIT)


def _resident(shape):
    n = len(shape)
    return pl.BlockSpec(shape, lambda *_: (0,) * n, pipeline_mode=pl.Buffered(1))


def _rms(x, w):
    return x * lax.rsqrt(jnp.mean(x * x, axis=-1, keepdims=True) + EPS) * w


def _silu(x):
    return x * jax.nn.sigmoid(x)


def _dot(a, b):
    return jnp.dot(a, b, preferred_element_type=F32)


def _dot_nt(a, b):
    return lax.dot_general(a, b, (((1,), (1,)), ((), ())), preferred_element_type=F32)


def _dot_tn(a, b):
    return lax.dot_general(a, b, (((0,), (0,)), ((), ())), preferred_element_type=F32)


def _cumsum_rows(tri, x):
    return jnp.dot(tri, x, preferred_element_type=F32, precision=lax.Precision.HIGHEST)


def _tri(t):
    r = lax.broadcasted_iota(jnp.int32, (t, t), 0)
    c = lax.broadcasted_iota(jnp.int32, (t, t), 1)
    return r >= c


def _gelu_tanh(x):
    c = 0.7978845608028654
    return 0.5 * x * (1.0 + jnp.tanh(c * (x + 0.044715 * (x * x * x))))


def _s5_kernel(x_ref, nw_ref, wu_ref, h0r_ref, h0i_ref, ar_ref, ai_ref, bbd_ref, cbd_ref, d_ref, gw_ref, gb_ref,
               perm_ref, o_ref, hr_ref, hi_ref, ubm, utm, xs0, xs1, ytm, st):
    nb, tt, _ = x_ref.shape
    half = S5_STATE * S5_GROUPS // S5_SLABS
    xs_bufs = (xs0, xs1)

    @pl.when(pl.program_id(0) == 0)
    def _():
        for j in range(S5_SLABS):
            st[2 * j] = h0r_ref[:, j * half:(j + 1) * half]
            st[2 * j + 1] = h0i_ref[:, j * half:(j + 1) * half]

    hn = _rms(x_ref[...].reshape(nb * tt, D_MODEL), nw_ref[...]).astype(BF16)
    ubm[...] = _dot(hn, wu_ref[...]).reshape(nb, tt, S5_WIDTH)
    for t in range(tt):
        utm[t] = ubm[:, t, :]

    def project(j):
        u_slab = utm[:, :, j * LANES:(j + 1) * LANES].reshape(tt * nb, LANES)
        xs_bufs[j % 2][...] = _dot(u_slab.astype(BF16), bbd_ref[j])

    project(0)
    for j in range(S5_SLABS):
        xs = xs_bufs[j % 2]
        if j + 1 < S5_SLABS:
            project(j + 1)
        ar = jnp.broadcast_to(ar_ref[:, j * half:(j + 1) * half], (nb, half))
        ai = jnp.broadcast_to(ai_ref[:, j * half:(j + 1) * half], (nb, half))
        sr, si = st[2 * j], st[2 * j + 1]
        for t in range(tt):
            rows = slice(t * nb, (t + 1) * nb)
            sr, si = (ar * sr - ai * si + xs[rows, 0:half], ar * si + ai * sr + xs[rows, half:2 * half])
            xs[rows, 0:half] = sr
            xs[rows, half:2 * half] = si
        st[2 * j] = sr
        st[2 * j + 1] = si
        hr_ref[:, j * half:(j + 1) * half] = sr
        hi_ref[:, j * half:(j + 1) * half] = si
        ytm[:, j * LANES:(j + 1) * LANES] = _dot(xs[...].astype(BF16), cbd_ref[j])

    u2 = utm[...].reshape(tt * nb, S5_WIDTH)

    y = ytm[...] + d_ref[...] * u2
    a = _gelu_tanh(y)
    out = a * jax.nn.sigmoid(_dot(a.astype(BF16), gw_ref[...]) + gb_ref[...])
    o_ref[...] = _dot(perm_ref[...], out.astype(BF16)).reshape(nb, tt, S5_WIDTH).astype(o_ref.dtype)


def _time_to_batch_major(tt, nb):
    dst = jnp.arange(tt * nb)
    src = (dst % tt) * nb + dst // tt
    return (src[:, None] == jnp.arange(tt * nb)[None, :]).astype(BF16)


def _s5_mixer(x, norm_w, w_u, h0r, h0i, prm):
    b, n, _ = x.shape
    tt = ROW_TILE // b
    nst = S5_GROUPS * S5_STATE
    half = nst // S5_SLABS
    return pl.pallas_call(
        _s5_kernel,
        grid=(n // tt,),
        in_specs=[pl.BlockSpec((b, tt, D_MODEL), lambda t: (0, t, 0)),
                  _resident((1, D_MODEL)), _resident((D_MODEL, S5_WIDTH)),
                  _resident((b, nst)), _resident((b, nst)),
                  _resident((1, nst)), _resident((1, nst)),
                  _resident((S5_SLABS, LANES, 2 * half)), _resident((S5_SLABS, 2 * half, LANES)),
                  _resident((1, S5_WIDTH)), _resident((S5_WIDTH, S5_WIDTH)), _resident((1, S5_WIDTH)),
                  _resident((tt * b, tt * b))],
        out_specs=(pl.BlockSpec((b, tt, S5_WIDTH), lambda t: (0, t, 0)),
                   pl.BlockSpec((b, nst), lambda t: (0, 0)),
                   pl.BlockSpec((b, nst), lambda t: (0, 0))),
        out_shape=(jax.ShapeDtypeStruct((b, n, S5_WIDTH), BF16),
                   jax.ShapeDtypeStruct((b, nst), F32),
                   jax.ShapeDtypeStruct((b, nst), F32)),
        scratch_shapes=[pltpu.VMEM((b, tt, S5_WIDTH), F32),
                        pltpu.VMEM((tt, b, S5_WIDTH), F32),
                        pltpu.VMEM((tt * b, 2 * half), F32), pltpu.VMEM((tt * b, 2 * half), F32),
                        pltpu.VMEM((tt * b, S5_WIDTH), F32),
                        pltpu.VMEM((2 * S5_SLABS, b, half), F32)],
        compiler_params=_cparams("arbitrary"),
        name="s5_mixer",
    )(x, norm_w, w_u, h0r, h0i, prm["ar"], prm["ai"], prm["bbd"], prm["cbd"], prm["d"], prm["glu_w"], prm["glu_b"],
      _time_to_batch_major(tt, b))


def _s5_params(lam_re, lam_im, log_dt, b_re, b_im, c_re, c_im, d_skip, glu_w, glu_b):
    dt = jnp.exp(log_dt)[:, None]
    mag = jnp.exp(lam_re * dt)
    ar, ai = mag * jnp.cos(lam_im * dt), mag * jnp.sin(lam_im * dt)
    den = lam_re * lam_re + lam_im * lam_im
    nr, ni = ar - 1.0, ai
    cr = (nr * lam_re + ni * lam_im) / den
    ci = (ni * lam_re - nr * lam_im) / den
    bbr = cr[..., None] * b_re - ci[..., None] * b_im
    bbi = cr[..., None] * b_im + ci[..., None] * b_re
    gps = S5_GROUPS // S5_SLABS
    eye = jnp.eye(gps, dtype=F32)

    def pack_b(bb):
        bb = bb.reshape(S5_SLABS, gps, S5_STATE, S5_GROUP)
        return jnp.einsum("sgph,gk->sghkp", bb, eye).reshape(S5_SLABS, gps * S5_GROUP, gps * S5_STATE)

    def pack_c(cc):
        cc = cc.reshape(S5_SLABS, gps, S5_GROUP, S5_STATE)
        return jnp.einsum("sghp,gk->sgpkh", cc, eye).reshape(S5_SLABS, gps * S5_STATE, gps * S5_GROUP)

    bbd = jnp.concatenate([pack_b(bbr), pack_b(bbi)], axis=2).astype(BF16)
    cbd = jnp.concatenate([pack_c(c_re), -pack_c(c_im)], axis=1).astype(BF16)
    return dict(ar=ar.reshape(1, -1), ai=ai.reshape(1, -1), bbd=bbd, cbd=cbd,
                d=d_skip.reshape(1, -1), glu_w=glu_w.astype(BF16), glu_b=glu_b.reshape(1, -1))


def _ssd_conv_stages(rows, cw_ref, cb_ref, buf_ref, cbuf, xact):
    for c0 in range(0, SSD_CONV_DIM, PROJ_COL):
        cs = slice(c0, c0 + PROJ_COL)
        acc = cb_ref[:, cs] + cw_ref[SSD_CONV - 1:SSD_CONV, cs] * cbuf[HALO:HALO + rows, cs]
        for j in range(SSD_CONV - 1):
            off = HALO - (SSD_CONV - 1) + j
            acc = acc + cw_ref[j:j + 1, cs] * cbuf[off:off + rows, cs]
        xact[:, cs] = _silu(acc)
        tail = cbuf[rows:rows + HALO, cs]
        cbuf[0:HALO, cs] = tail
        buf_ref[0, :, cs] = tail
        yield CONV_COST * rows // MIX_CHUNK


def _ssd_stages(rows, z_ref, dt_ref, dtb_ref, alog_ref, dsk_ref, nw_ref, yb_ref, xact_ref, ht, ybuf, xwbuf):
    t = rows.stop - rows.start
    gs = SSD_HPG * SSD_HEAD_DIM
    xact = xact_ref.at[rows]
    dt = jax.nn.softplus(dt_ref[rows, :] + dtb_ref[...])
    la = dt * (-LOG2E * jnp.exp(alog_ref[...]))
    mask = _tri(t)
    cum = _cumsum_rows(mask.astype(F32), la)
    cum_t = cum.T
    last = cum[t - 1:t, :]
    ecum = jnp.exp2(cum)
    wgt = jnp.exp2(last - cum) * dt
    elast = jnp.exp2(last)
    yield 150

    b0 = SSD_INNER
    c0 = SSD_INNER + SSD_GROUPS * SSD_STATE
    for g in range(SSD_GROUPS):
        bg = xact[:, b0 + g * SSD_STATE:b0 + (g + 1) * SSD_STATE].astype(BF16)
        cg = xact[:, c0 + g * SSD_STATE:c0 + (g + 1) * SSD_STATE].astype(BF16)
        scores = _dot_nt(cg, bg)
        hg = ht[:, g * gs:(g + 1) * gs]
        yint = _dot(cg, hg.astype(BF16))
        for r in range(SSD_HPG):
            h = g * SSD_HPG + r
            hs = slice(h * SSD_HEAD_DIM, (h + 1) * SSD_HEAD_DIM)
            xh = xact[:, hs]
            seg = cum[:, h:h + 1] - cum_t[h:h + 1, :]
            decay = jnp.exp2(jnp.where(mask, seg, -jnp.inf))
            m = (scores * decay).astype(BF16)
            y = _dot(m, (dt[:, h:h + 1] * xh).astype(BF16))
            y = y + yint[:, r * SSD_HEAD_DIM:(r + 1) * SSD_HEAD_DIM] * ecum[:, h:h + 1]
            ybuf[:, hs] = y
            xwbuf[:, hs] = (wgt[:, h:h + 1] * xh).astype(BF16)
            ht[:, hs] = hg[:, r * SSD_HEAD_DIM:(r + 1) * SSD_HEAD_DIM] * elast[:, h:h + 1]
        ht[:, g * gs:(g + 1) * gs] += _dot_tn(bg, xwbuf[:, g * gs:(g + 1) * gs])
        yield 350

    y = ybuf[...] + dsk_ref[...] * xact[:, :SSD_INNER]
    yb_ref[0, rows, :] = _rms(y * _silu(z_ref[rows, :]), nw_ref[...]).astype(yb_ref.dtype)
    yield 300


def _pad_lanes(v):
    return jnp.pad(v, (0, LANES - v.shape[0])).reshape(1, LANES)


def _ssd_params(conv_w, conv_b, dt_bias, a_log, d_skip, norm_w):
    return dict(conv_w=conv_w, conv_b=conv_b.reshape(1, -1), dt_bias=_pad_lanes(dt_bias), a_log=_pad_lanes(a_log),
                d=jnp.repeat(d_skip, SSD_HEAD_DIM).reshape(1, -1), norm_w=norm_w.reshape(1, -1))


def _hgrn_stages(rows, q_ref, f_ref, i_ref, g_ref, lb, nw, yc_ref, st, att_s):
    t = rows.stop - rows.start
    nsub = t // HG_SUB
    k = HG_EXPAND
    q = q_ref[rows, :]
    fz = f_ref[rows, :]
    e = jnp.exp(-jnp.abs(fz))
    lsp = jnp.minimum(fz, 0.0) - jnp.log(1.0 + e)
    lsn = lsp - fz
    lbt = jnp.log(lb) + lsn
    log_f = jnp.maximum(lsp, lbt) + jnp.log(1.0 + jnp.exp(-jnp.abs(lsp - lbt)))
    kk = (1.0 - lb) * (jnp.where(fz >= 0.0, e, 1.0) / (1.0 + e))

    cum = _cumsum_rows(_tri(t).astype(F32), log_f * LOG2E)
    last = cum[t - 1:t, :]
    qe = (q * jnp.exp2(cum)).astype(BF16)
    ke = (kk * jnp.exp2(last - cum)).astype(BF16)
    elast = jnp.exp2(last)
    yield 350

    c3 = cum.reshape(nsub, HG_SUB, HG_WIDTH)
    q3 = q.reshape(nsub, HG_SUB, HG_WIDTH)
    rrep = jnp.broadcast_to(c3[:, HG_SUB - 1:HG_SUB, :], c3.shape)
    kl = (kk * jnp.exp2((rrep - c3).reshape(t, HG_WIDTH))).astype(BF16)
    pairs = [(i, j) for i in range(1, nsub) for j in range(i)]
    if pairs:
        lhs = jnp.concatenate([q3[i] * jnp.exp2(c3[i] - rrep[j]) for i, j in pairs], axis=0).astype(BF16)
    col = lax.broadcasted_iota(jnp.int32, (HG_SUB, t), 1)
    colblk = col // HG_SUB
    row1 = lax.broadcasted_iota(jnp.int32, (HG_SUB, 1), 0)
    diag = col - row1

    for h in range(HG_HEADS):
        hs = slice(h * k, (h + 1) * k)
        if pairs:
            blk = _dot_nt(lhs[:, hs], kl[:, hs])
        for i in range(nsub):
            acc = jnp.zeros((HG_SUB, t), F32)
            for p, (pi, pj) in enumerate(pairs):
                if pi == i:
                    acc = jnp.where(colblk == pj, blk[p * HG_SUB:(p + 1) * HG_SUB, :], acc)
            att_s[h, i * HG_SUB:(i + 1) * HG_SUB, :] = acc
    yield 150

    for i in range(nsub):
        sub = slice(i * HG_SUB, (i + 1) * HG_SUB)
        qi, ci, ki = q[sub], cum[sub], kk[sub]
        accs = [att_s[h, sub, :] for h in range(HG_HEADS)]
        for d in range(HG_SUB):
            if d:
                p = qi * jnp.exp2(jnp.minimum(ci - pltpu.roll(ci, d, 0), 0.0)) * pltpu.roll(ki, d, 0)
            else:
                p = qi * ki
            hit = jnp.where(row1 >= d, diag, -2 * t) == i * HG_SUB - d
            for h in range(HG_HEADS):
                a = jnp.sum(p[:, h * k:(h + 1) * k], axis=-1, keepdims=True)
                accs[h] = jnp.where(hit, a, accs[h])
        for h in range(HG_HEADS):
            att_s[h, sub, :] = accs[h]
        if i % 2 == 1:
            yield 80

    iv = i_ref[rows, :].astype(BF16)
    for h in range(HG_HEADS):
        hs = slice(h * k, (h + 1) * k)
        sh = st[h]
        o = _dot(att_s[h].astype(BF16), iv[:, hs]) + _dot_nt(qe[:, hs], sh.astype(BF16))
        st[h] = sh * elast[:, hs] + _dot_tn(iv[:, hs], ke[:, hs])
        yc_ref[0, rows, hs] = (_rms(o, nw) * _silu(g_ref[rows, hs])).astype(yc_ref.dtype)
        if h % 2 == 1:
            yield 150


_MIX_COLS = {}
_off = 0
for _name, _width in (("z", SSD_INNER), ("xbc", SSD_CONV_DIM), ("q", HG_WIDTH), ("f", HG_WIDTH), ("i", HG_WIDTH),
                      ("g", HG_WIDTH), ("gates", 3 * D_MODEL), ("dt", LANES)):
    _MIX_COLS[_name] = (_off, _width)
    _off += _width
_MIX_WIDTH = _off


def _mix_kernel(x_ref, nw_ref, w_ref, h0_ref, cache_ref, s0_ref,
                cw_ref, cb_ref, dtb_ref, alog_ref, dsk_ref, snw_ref, lb_ref, hnw_ref,
                yb_ref, yc_ref, gates_ref, hn_ref, buf_ref, sn_ref,
                cbuf, xact, ht, ybuf, xwbuf, st, att_s, pz, pdt, pq, pf, pi, pg, hnb):
    rows = x_ref.shape[1]
    c = pl.program_id(1)
    gs = SSD_HPG * SSD_HEAD_DIM
    k = HG_EXPAND

    @pl.when(c == 0)
    def _():
        cbuf[0:HALO, :] = cache_ref[0]
        for g in range(SSD_GROUPS):
            ht[:, g * gs:(g + 1) * gs] = h0_ref[0, g * gs:(g + 1) * gs, :].T
        for h in range(HG_HEADS):
            st[h] = s0_ref[0, h * k:(h + 1) * k, :].T

    hnb[...] = _rms(x_ref[0], nw_ref[...]).astype(BF16)

    def piece(name, lo, width, store):
        off, _ = _MIX_COLS[name]
        return lambda: store(_dot(hnb[...], w_ref[:, off + lo:off + lo + width]), lo, width)

    def into(ref, row0=0):
        def store(v, lo, width):
            ref[row0:row0 + rows, lo:lo + width] = v.astype(ref.dtype)
        return store

    def into_gates(v, lo, width):
        gates_ref[0, :, lo:lo + width] = v.astype(gates_ref.dtype)

    pieces = [piece("xbc", lo, PROJ_COL, into(cbuf, HALO)) for lo in range(0, SSD_CONV_DIM, PROJ_COL)]
    pieces.insert(1, piece("dt", 0, LANES, into(pdt)))
    pieces += [piece("z", lo, PROJ_COL, into(pz)) for lo in range(0, SSD_INNER, PROJ_COL)]
    pieces += [piece(name, lo, PROJ_COL, into(ref)) for name, ref in (("q", pq), ("f", pf), ("i", pi), ("g", pg))
               for lo in range(0, HG_WIDTH, PROJ_COL)]
    pieces += [piece("gates", lo, PROJ_COL, into_gates) for lo in range(0, 3 * D_MODEL, PROJ_COL)]

    def issue(n=1):
        for _ in range(min(n, len(pieces))):
            pieces.pop(0)()

    issue(3)
    ts = min(rows, MIX_CHUNK)
    th = min(rows, HG_CHUNK)
    stages = [_ssd_conv_stages(rows, cw_ref, cb_ref, buf_ref, cbuf, xact)]
    stages += [_ssd_stages(slice(r0, r0 + ts), pz, pdt, dtb_ref, alog_ref, dsk_ref, snw_ref, yb_ref,
                           xact, ht, ybuf, xwbuf) for r0 in range(0, rows, ts)]
    stages += [_hgrn_stages(slice(r0, r0 + th), pq, pf, pi, pg, lb_ref[...], hnw_ref[...], yc_ref, st, att_s)
               for r0 in range(0, rows, th)]
    piece_cost = PIECE_COST * rows // MIX_CHUNK
    credit = 0
    for gen in stages:
        for weight in gen:
            credit += weight
            while credit >= piece_cost and pieces:
                issue()
                credit -= piece_cost
    issue(len(pieces))

    @pl.when(c == pl.num_programs(1) - 1)
    def _():
        for g in range(SSD_GROUPS):
            hn_ref[0, g * gs:(g + 1) * gs, :] = ht[:, g * gs:(g + 1) * gs].T
        for h in range(HG_HEADS):
            sn_ref[0, h * k:(h + 1) * k, :] = st[h].T


def _mix(x, norm_w, w_mix, ssd_h0, ssd_cache, hg_s0, ssd, hg_lb, hg_norm_w):
    b, n, _ = x.shape
    t = min(n, MIX_ROWS)
    ts = min(t, MIX_CHUNK)
    th = min(t, HG_CHUNK)
    srows = SSD_HEADS * SSD_HEAD_DIM
    hrows = HG_HEADS * HG_EXPAND
    tok = lambda w: pl.BlockSpec((1, t, w), lambda i, c: (i, c, 0))
    per_seq = lambda r, w: pl.BlockSpec((1, r, w), lambda i, c: (i, 0, 0))
    return pl.pallas_call(
        _mix_kernel,
        grid=(b, n // t),
        in_specs=[tok(D_MODEL), _resident((1, D_MODEL)), _resident((D_MODEL, _MIX_WIDTH)),
                  per_seq(srows, SSD_STATE), per_seq(HALO, SSD_CONV_DIM), per_seq(hrows, HG_EXPAND),
                  _resident((SSD_CONV, SSD_CONV_DIM)), _resident((1, SSD_CONV_DIM)),
                  _resident((1, LANES)), _resident((1, LANES)),
                  _resident((1, SSD_INNER)), _resident((1, SSD_INNER)),
                  _resident((1, HG_WIDTH)), _resident((1, HG_EXPAND))],
        out_specs=(tok(SSD_INNER), tok(HG_WIDTH), tok(3 * D_MODEL),
                   per_seq(srows, SSD_STATE), per_seq(HALO, SSD_CONV_DIM), per_seq(hrows, HG_EXPAND)),
        out_shape=(jax.ShapeDtypeStruct((b, n, SSD_INNER), BF16),
                   jax.ShapeDtypeStruct((b, n, HG_WIDTH), BF16),
                   jax.ShapeDtypeStruct((b, n, 3 * D_MODEL), BF16),
                   jax.ShapeDtypeStruct((b, srows, SSD_STATE), F32),
                   jax.ShapeDtypeStruct((b, HALO, SSD_CONV_DIM), F32),
                   jax.ShapeDtypeStruct((b, hrows, HG_EXPAND), F32)),
        scratch_shapes=[pltpu.VMEM((t + HALO, SSD_CONV_DIM), F32),
                        pltpu.VMEM((t, SSD_CONV_DIM), F32),
                        pltpu.VMEM((SSD_STATE, srows), F32),
                        pltpu.VMEM((ts, SSD_INNER), F32),
                        pltpu.VMEM((ts, SSD_INNER), BF16),
                        pltpu.VMEM((HG_HEADS, HG_EXPAND, HG_EXPAND), F32),
                        pltpu.VMEM((HG_HEADS, th, th), F32),
                        pltpu.VMEM((t, SSD_INNER), F32),
                        pltpu.VMEM((t, LANES), F32),
                        pltpu.VMEM((t, HG_WIDTH), F32), pltpu.VMEM((t, HG_WIDTH), F32),
                        pltpu.VMEM((t, HG_WIDTH), F32), pltpu.VMEM((t, HG_WIDTH), F32),
                        pltpu.VMEM((t, D_MODEL), BF16)],
        compiler_params=_cparams("parallel", "arbitrary"),
        name="mix",
    )(x, norm_w, w_mix, ssd_h0, ssd_cache, hg_s0, ssd["conv_w"], ssd["conv_b"], ssd["dt_bias"], ssd["a_log"],
      ssd["d"], ssd["norm_w"], hg_lb, hg_norm_w)


def _merge_kernel(x_ref, ya_ref, yb_ref, yc_ref, gl_ref, wa_ref, wb_ref, wc_ref, wo_ref, o_ref):
    d = D_MODEL
    merged = jax.nn.sigmoid(gl_ref[:, 0:d].astype(F32)) * _dot(ya_ref[...], wa_ref[...])
    merged = merged + jax.nn.sigmoid(gl_ref[:, d:2 * d].astype(F32)) * _dot(yb_ref[...], wb_ref[...])
    merged = merged + jax.nn.sigmoid(gl_ref[:, 2 * d:3 * d].astype(F32)) * _dot(yc_ref[...], wc_ref[...])
    o_ref[...] = x_ref[...] + _dot(merged.astype(BF16), wo_ref[...])


def _merge(x2d, ya, yb, yc, gl, wa, wb, wc, wo):
    m = x2d.shape[0]
    tm = min(m, ROW_TILE)
    row = lambda w: pl.BlockSpec((tm, w), lambda i: (i, 0))
    return pl.pallas_call(
        _merge_kernel,
        grid=(m // tm,),
        in_specs=[row(D_MODEL), row(S5_WIDTH), row(SSD_INNER), row(HG_WIDTH), row(3 * D_MODEL),
                  _resident(wa.shape), _resident(wb.shape), _resident(wc.shape), _resident(wo.shape)],
        out_specs=row(D_MODEL),
        out_shape=jax.ShapeDtypeStruct((m, D_MODEL), F32),
        compiler_params=_cparams("parallel"),
        name="merge",
    )(x2d, ya, yb, yc, gl, wa, wb, wc, wo)


def _ffn_kernel(x_ref, cache_ref, nw_ref, wup_ref, cw_ref, cb_ref, wdn_ref, fw_ref,
                o_ref, buf_ref, abuf, gbuf, *, final_norm):
    nb, tt, _ = x_ref.shape

    @pl.when(pl.program_id(1) == 0)
    def _():
        abuf[:, 0:HALO, :] = cache_ref[...]

    x = x_ref[...].reshape(nb * tt, D_MODEL)
    hn = _rms(x, nw_ref[...]).astype(BF16)
    for c0 in range(0, FFN_DIM, FFN_COL):
        cs = slice(c0, c0 + FFN_COL)
        a = _dot(hn, wup_ref[:, cs])
        v = _dot(hn, wup_ref[:, FFN_DIM + c0:FFN_DIM + c0 + FFN_COL])
        abuf[:, HALO:HALO + tt, cs] = a.reshape(nb, tt, FFN_COL)
        conv = cb_ref[:, cs] + cw_ref[FFN_CONV - 1:FFN_CONV, cs] * a.reshape(nb, tt, FFN_COL)
        for j in range(FFN_CONV - 1):
            off = HALO - (FFN_CONV - 1) + j
            conv = conv + cw_ref[j:j + 1, cs] * abuf[:, off:off + tt, cs]
        gbuf[:, cs] = (_silu(conv).reshape(nb * tt, FFN_COL) * v).astype(BF16)
    tail = abuf[:, tt:tt + HALO, :]
    abuf[:, 0:HALO, :] = tail
    buf_ref[...] = tail
    y = x + _dot(gbuf[...], wdn_ref[...])
    if final_norm:
        y = _rms(y, fw_ref[...])
    o_ref[...] = y.reshape(nb, tt, D_MODEL)


def _ffn(x, cache, norm_w, w_up, conv_w, conv_b, w_down, final_w, final_norm):
    b, n, _ = x.shape
    tt = min(n, ROW_TILE)
    nb = ROW_TILE // tt
    return pl.pallas_call(
        functools.partial(_ffn_kernel, final_norm=final_norm),
        grid=(b // nb, n // tt),
        in_specs=[pl.BlockSpec((nb, tt, D_MODEL), lambda i, c: (i, c, 0)),
                  pl.BlockSpec((nb, HALO, FFN_DIM), lambda i, c: (i, 0, 0)),
                  _resident((1, D_MODEL)), _resident((D_MODEL, 2 * FFN_DIM)),
                  _resident((FFN_CONV, FFN_DIM)), _resident((1, FFN_DIM)),
                  _resident((FFN_DIM, D_MODEL)), _resident((1, D_MODEL))],
        out_specs=(pl.BlockSpec((nb, tt, D_MODEL), lambda i, c: (i, c, 0)),
                   pl.BlockSpec((nb, HALO, FFN_DIM), lambda i, c: (i, 0, 0))),
        out_shape=(jax.ShapeDtypeStruct((b, n, D_MODEL), F32),
                   jax.ShapeDtypeStruct((b, HALO, FFN_DIM), F32)),
        scratch_shapes=[pltpu.VMEM((nb, tt + HALO, FFN_DIM), F32),
                        pltpu.VMEM((nb * tt, FFN_DIM), BF16)],
        compiler_params=_cparams("parallel", "arbitrary"),
        name="ffn",
    )(x, cache, norm_w, w_up, conv_w, conv_b, w_down, final_w)


def _pad_cache(cache):
    return jnp.pad(cache, ((0, 0), (HALO - cache.shape[1], 0), (0, 0)))


def _split_w_in(w_in):
    sizes = (S5_WIDTH, SSD_INNER, SSD_CONV_DIM, SSD_HEADS, HG_WIDTH, HG_WIDTH, HG_WIDTH, HG_WIDTH, 3 * D_MODEL)
    offs = [0]
    for s in sizes:
        offs.append(offs[-1] + s)
    u, z, xbc, dt, q, f, i, g, gates = (w_in[:, offs[k]:offs[k + 1]] for k in range(len(sizes)))
    dt = jnp.pad(dt, ((0, 0), (0, LANES - SSD_HEADS)))
    return u.astype(BF16), jnp.concatenate([z, xbc, q, f, i, g, gates, dt], axis=1).astype(BF16)


def _run_trunk(x, s5_re, s5_im, ssd_h, ssd_buf, hg_s, ffn_buf, layers, final_w):
    b, n, _ = x.shape
    m = b * n
    outs = ([], [], [], [], [], [])
    for l, w in enumerate(layers):
        yb, yc, gates, hb, bufb, sc = _mix(x, w["norm_mix_w"], w["w_mix"], ssd_h[l].reshape(b, -1, SSD_STATE),
                                           _pad_cache(ssd_buf[l]), hg_s[l].reshape(b, -1, HG_EXPAND),
                                           w["ssd"], w["hg_lb"], w["hg_norm_w"])
        ya, hr, hi = _s5_mixer(x, w["norm_mix_w"], w["w_u"], s5_re[l].reshape(b, -1), s5_im[l].reshape(b, -1), w["s5"])
        h2d = _merge(x.reshape(m, D_MODEL), ya.reshape(m, -1), yb.reshape(m, -1), yc.reshape(m, -1),
                     gates.reshape(m, -1), w["w_branch_a"], w["w_branch_b"], w["w_branch_c"], w["w_out"])
        x, fb = _ffn(h2d.reshape(b, n, D_MODEL), _pad_cache(ffn_buf[l]), w["norm_ffn_w"], w["ffn_w_up"],
                     w["ffn_conv_w"], w["ffn_conv_b"], w["ffn_w_down"], final_w, l == len(layers) - 1)
        vals = (hr.reshape(b, S5_GROUPS, S5_STATE), hi.reshape(b, S5_GROUPS, S5_STATE),
                hb.reshape(b, SSD_HEADS, SSD_HEAD_DIM, SSD_STATE), bufb[:, HALO - (SSD_CONV - 1):],
                sc.reshape(b, HG_HEADS, HG_EXPAND, HG_EXPAND), fb[:, HALO - (FFN_CONV - 1):])
        for lst, val in zip(outs, vals):
            lst.append(val)
    return (x,) + tuple(jnp.stack(v) for v in outs)


def kernel(x_prompt, x_sample, state_s5_re, state_s5_im, state_ssd, cache_ssd_conv, state_hgrn, cache_ffn_conv,
           norm_mix_w, w_in, s5_lambda_re, s5_lambda_im, s5_log_dt, s5_b_re, s5_b_im, s5_c_re, s5_c_im,
           s5_d, s5_glu_w, s5_glu_b, ssd_conv_w, ssd_conv_b, ssd_dt_bias, ssd_a_log, ssd_d, ssd_norm_w,
           hg_lb_logits, hg_norm_w, w_branch_a, w_branch_b, w_branch_c, w_out,
           norm_ffn_w, ffn_w_up, ffn_conv_w, ffn_conv_b, ffn_w_down, norm_final_w):
    lb_cum = jnp.cumsum(jax.nn.softmax(hg_lb_logits.astype(F32), axis=0), axis=0)
    hg_lb = lb_cum - lb_cum[0]
    layers = []
    for l in range(DEPTH):
        w_u, w_mix = _split_w_in(w_in[l])
        layers.append(dict(
            norm_mix_w=norm_mix_w[l].reshape(1, -1), w_u=w_u, w_mix=w_mix,
            s5=_s5_params(s5_lambda_re[l], s5_lambda_im[l], s5_log_dt[l], s5_b_re[l], s5_b_im[l],
                          s5_c_re[l], s5_c_im[l], s5_d[l], s5_glu_w[l], s5_glu_b[l]),
            ssd=_ssd_params(ssd_conv_w[l], ssd_conv_b[l], ssd_dt_bias[l], ssd_a_log[l], ssd_d[l], ssd_norm_w[l]),
            hg_lb=hg_lb[l].reshape(1, -1), hg_norm_w=hg_norm_w[l].reshape(1, -1),
            w_branch_a=w_branch_a[l].astype(BF16), w_branch_b=w_branch_b[l].astype(BF16),
            w_branch_c=w_branch_c[l].astype(BF16), w_out=w_out[l].astype(BF16),
            norm_ffn_w=norm_ffn_w[l].reshape(1, -1), ffn_w_up=ffn_w_up[l].astype(BF16),
            ffn_conv_w=ffn_conv_w[l], ffn_conv_b=ffn_conv_b[l].reshape(1, -1),
            ffn_w_down=ffn_w_down[l].astype(BF16)))
    final_w = norm_final_w.reshape(1, -1)
    bp = x_prompt.shape[0]
    zeros = lambda *s: jnp.zeros((DEPTH, bp) + s, F32)
    p = _run_trunk(x_prompt, zeros(S5_GROUPS, S5_STATE), zeros(S5_GROUPS, S5_STATE),
                   zeros(SSD_HEADS, SSD_HEAD_DIM, SSD_STATE), zeros(SSD_CONV - 1, SSD_CONV_DIM),
                   zeros(HG_HEADS, HG_EXPAND, HG_EXPAND), zeros(FFN_CONV - 1, FFN_DIM), layers, final_w)
    s = _run_trunk(x_sample, state_s5_re, state_s5_im, state_ssd, cache_ssd_conv, state_hgrn, cache_ffn_conv,
                   layers, final_w)
    return (p[0], s[0]) + p[1:] + s[1:]
```

```python
import functools

import jax
import jax.numpy as jnp
from jax import lax
from jax.experimental import pallas as pl
from jax.experimental.pallas import tpu as pltpu

F32 = jnp.float32
BF16 = jnp.bfloat16
EPS = 1e-6

D_MODEL = 1024
DEPTH = 2
S5_WIDTH = 512
S5_GROUP = 16
S5_GROUPS = 32
S5_STATE = 64
S5_SLABS = 4
SSD_INNER = 1024
SSD_HEAD_DIM = 64
SSD_HEADS = 16
SSD_GROUPS = 4
SSD_HPG = 4
SSD_STATE = 128
SSD_CONV = 4
SSD_CONV_DIM = 2048
HG_WIDTH = 512
HG_EXPAND = 128
HG_HEADS = 4
HG_CHUNK = 64
HG_SUB = 8
LOG2E = 1.4426950408889634
MIX_CHUNK = 128
MIX_ROWS = 256
PROJ_COL = 256
PIECE_COST = PROJ_COL * 42 // 64
CONV_COST = PROJ_COL * 25 // 32
FFN_DIM = 2816
FFN_CONV = 3
FFN_COL = 256
LANES = 128
HALO = 8
ROW_TILE = 512
VMEM_LIMIT = 56 * 1024 * 1024


def _cparams(*sem):
    return pltpu.CompilerParams(dimension_semantics=sem, vmem_limit_bytes=VMEM_LIMIT)


def _resident(shape):
    n = len(shape)
    return pl.BlockSpec(shape, lambda *_: (0,) * n, pipeline_mode=pl.Buffered(1))


def _layer(stacked, l):
    n = stacked.ndim - 1
    spec = pl.BlockSpec((None,) + stacked.shape[1:], lambda *_: (l,) + (0,) * n, pipeline_mode=pl.Buffered(1))
    return stacked, spec


def _rms(x, w):
    return x * lax.rsqrt(jnp.mean(x * x, axis=-1, keepdims=True) + EPS) * w


def _silu(x):
    return x * jax.nn.sigmoid(x)


def _dot(a, b):
    return jnp.dot(a, b, preferred_element_type=F32)


def _dot_nt(a, b):
    return lax.dot_general(a, b, (((1,), (1,)), ((), ())), preferred_element_type=F32)


def _dot_tn(a, b):
    return lax.dot_general(a, b, (((0,), (0,)), ((), ())), preferred_element_type=F32)


def _cumsum_rows(tri, x):
    return jnp.dot(tri, x, preferred_element_type=F32, precision=lax.Precision.HIGHEST)


def _tri(t):
    r = lax.broadcasted_iota(jnp.int32, (t, t), 0)
    c = lax.broadcasted_iota(jnp.int32, (t, t), 1)
    return r >= c


def _gelu_tanh(x):
    c = 0.7978845608028654
    return 0.5 * x * (1.0 + jnp.tanh(c * (x + 0.044715 * (x * x * x))))


def _s5_kernel(x_ref, yb_ref, yc_ref, gl_ref, nw_ref, wu_ref, h0r_ref, h0i_ref, ar_ref, ai_ref, bbd_ref, cbd_ref,
               d_ref, gw_ref, gb_ref, perm_ref, wa_ref, wb_ref, wc_ref, wo_ref,
               o_ref, hr_ref, hi_ref, ubm, utm, xs0, xs1, ytm, st, macc):
    nb, tt, _ = x_ref.shape
    half = S5_STATE * S5_GROUPS // S5_SLABS
    xs_bufs = (xs0, xs1)
    m = nb * tt
    d = D_MODEL
    mq = d // S5_SLABS

    def gate(k, cols):
        return jax.nn.sigmoid(gl_ref[:, :, k * d + cols.start:k * d + cols.stop].reshape(m, mq).astype(F32))

    @pl.when(pl.program_id(0) == 0)
    def _():
        for j in range(S5_SLABS):
            st[2 * j] = h0r_ref[:, j * half:(j + 1) * half]
            st[2 * j + 1] = h0i_ref[:, j * half:(j + 1) * half]

    hn = _rms(x_ref[...].reshape(nb * tt, D_MODEL), nw_ref[...]).astype(BF16)
    ubm[...] = _dot(hn, wu_ref[...]).reshape(nb, tt, S5_WIDTH)
    for t in range(tt):
        utm[t] = ubm[:, t, :]

    def project(j):
        u_slab = utm[:, :, j * LANES:(j + 1) * LANES].reshape(tt * nb, LANES)
        xs_bufs[j % 2][...] = _dot(u_slab.astype(BF16), bbd_ref[j])

    project(0)
    for j in range(S5_SLABS):
        xs = xs_bufs[j % 2]
        if j + 1 < S5_SLABS:
            project(j + 1)
        cols = slice(j * mq, (j + 1) * mq)
        macc[:, cols] = (gate(1, cols) * _dot(yb_ref[...].reshape(m, SSD_INNER), wb_ref[:, cols])
                         + gate(2, cols) * _dot(yc_ref[...].reshape(m, HG_WIDTH), wc_ref[:, cols]))
        ar = jnp.broadcast_to(ar_ref[:, j * half:(j + 1) * half], (nb, half))
        ai = jnp.broadcast_to(ai_ref[:, j * half:(j + 1) * half], (nb, half))
        sr, si = st[2 * j], st[2 * j + 1]
        for t in range(tt):
            rows = slice(t * nb, (t + 1) * nb)
            sr, si = (ar * sr - ai * si + xs[rows, 0:half], ar * si + ai * sr + xs[rows, half:2 * half])
            xs[rows, 0:half] = sr
            xs[rows, half:2 * half] = si
        st[2 * j] = sr
        st[2 * j + 1] = si
        hr_ref[:, j * half:(j + 1) * half] = sr
        hi_ref[:, j * half:(j + 1) * half] = si
        ytm[:, j * LANES:(j + 1) * LANES] = _dot(xs[...].astype(BF16), cbd_ref[j])

    u2 = utm[...].reshape(tt * nb, S5_WIDTH)

    y = ytm[...] + d_ref[...] * u2
    a = _gelu_tanh(y)
    out = a * jax.nn.sigmoid(_dot(a.astype(BF16), gw_ref[...]) + gb_ref[...])
    ya = _dot(perm_ref[...], out.astype(BF16)).astype(BF16)
    for j in range(S5_SLABS):
        cols = slice(j * mq, (j + 1) * mq)
        macc[:, cols] += gate(0, cols) * _dot(ya, wa_ref[:, cols])
    h = x_ref[...].reshape(m, d) + _dot(macc[...].astype(BF16), wo_ref[...])
    o_ref[...] = h.reshape(nb, tt, d)


def _time_to_batch_major(tt, nb):
    dst = jnp.arange(tt * nb)
    src = (dst % tt) * nb + dst // tt
    return (src[:, None] == jnp.arange(tt * nb)[None, :]).astype(BF16)


def _s5_merge(x, yb, yc, gates, norm_w, w_u, h0r, h0i, prm, wa, wb, wc, wo):
    b, n, _ = x.shape
    tt = ROW_TILE // b
    nst = S5_GROUPS * S5_STATE
    half = nst // S5_SLABS
    tok = lambda w: pl.BlockSpec((b, tt, w), lambda t: (0, t, 0))
    return pl.pallas_call(
        _s5_kernel,
        grid=(n // tt,),
        in_specs=[tok(D_MODEL), tok(SSD_INNER), tok(HG_WIDTH), tok(3 * D_MODEL),
                  _resident((1, D_MODEL)), w_u[1],
                  _resident((b, nst)), _resident((b, nst)),
                  _resident((1, nst)), _resident((1, nst)),
                  _resident((S5_SLABS, LANES, 2 * half)), _resident((S5_SLABS, 2 * half, LANES)),
                  _resident((1, S5_WIDTH)), prm["glu_w"][1], _resident((1, S5_WIDTH)),
                  _resident((tt * b, tt * b)), wa[1], wb[1], wc[1], wo[1]],
        out_specs=(tok(D_MODEL),
                   pl.BlockSpec((b, nst), lambda t: (0, 0)),
                   pl.BlockSpec((b, nst), lambda t: (0, 0))),
        out_shape=(jax.ShapeDtypeStruct((b, n, D_MODEL), F32),
                   jax.ShapeDtypeStruct((b, nst), F32),
                   jax.ShapeDtypeStruct((b, nst), F32)),
        scratch_shapes=[pltpu.VMEM((b, tt, S5_WIDTH), F32),
                        pltpu.VMEM((tt, b, S5_WIDTH), F32),
                        pltpu.VMEM((tt * b, 2 * half), F32), pltpu.VMEM((tt * b, 2 * half), F32),
                        pltpu.VMEM((tt * b, S5_WIDTH), F32),
                        pltpu.VMEM((2 * S5_SLABS, b, half), F32),
                        pltpu.VMEM((tt * b, D_MODEL), F32)],
        compiler_params=_cparams("arbitrary"),
        name="s5_merge",
    )(x, yb, yc, gates, norm_w, w_u[0], h0r, h0i, prm["ar"], prm["ai"], prm["bbd"], prm["cbd"], prm["d"],
      prm["glu_w"][0], prm["glu_b"], _time_to_batch_major(tt, b), wa[0], wb[0], wc[0], wo[0])


def _s5_params(lam_re, lam_im, log_dt, b_re, b_im, c_re, c_im, d_skip, glu_w, glu_b):
    dt = jnp.exp(log_dt)[:, None]
    mag = jnp.exp(lam_re * dt)
    ar, ai = mag * jnp.cos(lam_im * dt), mag * jnp.sin(lam_im * dt)
    den = lam_re * lam_re + lam_im * lam_im
    nr, ni = ar - 1.0, ai
    cr = (nr * lam_re + ni * lam_im) / den
    ci = (ni * lam_re - nr * lam_im) / den
    bbr = cr[..., None] * b_re - ci[..., None] * b_im
    bbi = cr[..., None] * b_im + ci[..., None] * b_re
    gps = S5_GROUPS // S5_SLABS
    eye = jnp.eye(gps, dtype=F32)

    def pack_b(bb):
        bb = bb.reshape(S5_SLABS, gps, S5_STATE, S5_GROUP)
        return jnp.einsum("sgph,gk->sghkp", bb, eye).reshape(S5_SLABS, gps * S5_GROUP, gps * S5_STATE)

    def pack_c(cc):
        cc = cc.reshape(S5_SLABS, gps, S5_GROUP, S5_STATE)
        return jnp.einsum("sghp,gk->sgpkh", cc, eye).reshape(S5_SLABS, gps * S5_STATE, gps * S5_GROUP)

    bbd = jnp.concatenate([pack_b(bbr), pack_b(bbi)], axis=2).astype(BF16)
    cbd = jnp.concatenate([pack_c(c_re), -pack_c(c_im)], axis=1).astype(BF16)
    return dict(ar=ar.reshape(1, -1), ai=ai.reshape(1, -1), bbd=bbd, cbd=cbd,
                d=d_skip.reshape(1, -1), glu_w=glu_w, glu_b=glu_b.reshape(1, -1))


def _ssd_conv_stages(rows, cw_ref, cb_ref, buf_ref, cbuf, xact):
    for c0 in range(0, SSD_CONV_DIM, PROJ_COL):
        cs = slice(c0, c0 + PROJ_COL)
        acc = cb_ref[:, cs] + cw_ref[SSD_CONV - 1:SSD_CONV, cs] * cbuf[HALO:HALO + rows, cs]
        for j in range(SSD_CONV - 1):
            off = HALO - (SSD_CONV - 1) + j
            acc = acc + cw_ref[j:j + 1, cs] * cbuf[off:off + rows, cs]
        xact[:, cs] = _silu(acc)
        tail = cbuf[rows:rows + HALO, cs]
        cbuf[0:HALO, cs] = tail
        buf_ref[0, :, cs] = tail
        yield CONV_COST * rows // MIX_CHUNK


def _ssd_stages(rows, z_ref, dt_ref, dtb_ref, alog_ref, dsk_ref, nw_ref, yb_ref, xact_ref, ht, ybuf, xwbuf):
    t = rows.stop - rows.start
    gs = SSD_HPG * SSD_HEAD_DIM
    xact = xact_ref.at[rows]
    dt = jax.nn.softplus(dt_ref[rows, :] + dtb_ref[...])
    la = dt * (-LOG2E * jnp.exp(alog_ref[...]))
    mask = _tri(t)
    cum = _cumsum_rows(mask.astype(F32), la)
    cum_t = cum.T
    last = cum[t - 1:t, :]
    ecum = jnp.exp2(cum)
    wgt = jnp.exp2(last - cum) * dt
    elast = jnp.exp2(last)
    yield 150

    b0 = SSD_INNER
    c0 = SSD_INNER + SSD_GROUPS * SSD_STATE
    for g in range(SSD_GROUPS):
        bg = xact[:, b0 + g * SSD_STATE:b0 + (g + 1) * SSD_STATE].astype(BF16)
        cg = xact[:, c0 + g * SSD_STATE:c0 + (g + 1) * SSD_STATE].astype(BF16)
        scores = _dot_nt(cg, bg)
        hg = ht[:, g * gs:(g + 1) * gs]
        yint = _dot(cg, hg.astype(BF16))
        for r in range(SSD_HPG):
            h = g * SSD_HPG + r
            hs = slice(h * SSD_HEAD_DIM, (h + 1) * SSD_HEAD_DIM)
            xh = xact[:, hs]
            seg = cum[:, h:h + 1] - cum_t[h:h + 1, :]
            decay = jnp.exp2(jnp.where(mask, seg, -jnp.inf))
            m = (scores * decay).astype(BF16)
            y = _dot(m, (dt[:, h:h + 1] * xh).astype(BF16))
            y = y + yint[:, r * SSD_HEAD_DIM:(r + 1) * SSD_HEAD_DIM] * ecum[:, h:h + 1]
            ybuf[:, hs] = y
            xwbuf[:, hs] = (wgt[:, h:h + 1] * xh).astype(BF16)
            ht[:, hs] = hg[:, r * SSD_HEAD_DIM:(r + 1) * SSD_HEAD_DIM] * elast[:, h:h + 1]
        ht[:, g * gs:(g + 1) * gs] += _dot_tn(bg, xwbuf[:, g * gs:(g + 1) * gs])
        yield 350

    y = ybuf[...] + dsk_ref[...] * xact[:, :SSD_INNER]
    yb_ref[0, rows, :] = _rms(y * _silu(z_ref[rows, :]), nw_ref[...]).astype(yb_ref.dtype)
    yield 300


def _pad_lanes(v):
    return jnp.pad(v, (0, LANES - v.shape[0])).reshape(1, LANES)


def _ssd_params(conv_w, conv_b, dt_bias, a_log, d_skip, norm_w):
    return dict(conv_w=conv_w, conv_b=conv_b.reshape(1, -1), dt_bias=_pad_lanes(dt_bias), a_log=_pad_lanes(a_log),
                d=jnp.repeat(d_skip, SSD_HEAD_DIM).reshape(1, -1), norm_w=norm_w.reshape(1, -1))


def _hgrn_stages(rows, q_ref, f_ref, i_ref, g_ref, lb, nw, yc_ref, st, att_s):
    t = rows.stop - rows.start
    nsub = t // HG_SUB
    k = HG_EXPAND
    q = q_ref[rows, :]
    fz = f_ref[rows, :]
    e = jnp.exp(-jnp.abs(fz))
    lsp = jnp.minimum(fz, 0.0) - jnp.log(1.0 + e)
    lsn = lsp - fz
    lbt = jnp.log(lb) + lsn
    log_f = jnp.maximum(lsp, lbt) + jnp.log(1.0 + jnp.exp(-jnp.abs(lsp - lbt)))
    kk = (1.0 - lb) * (jnp.where(fz >= 0.0, e, 1.0) / (1.0 + e))

    cum = _cumsum_rows(_tri(t).astype(F32), log_f * LOG2E)
    last = cum[t - 1:t, :]
    qe = (q * jnp.exp2(cum)).astype(BF16)
    ke = (kk * jnp.exp2(last - cum)).astype(BF16)
    elast = jnp.exp2(last)
    yield 350

    c3 = cum.reshape(nsub, HG_SUB, HG_WIDTH)
    q3 = q.reshape(nsub, HG_SUB, HG_WIDTH)
    rrep = jnp.broadcast_to(c3[:, HG_SUB - 1:HG_SUB, :], c3.shape)
    kl = (kk * jnp.exp2((rrep - c3).reshape(t, HG_WIDTH))).astype(BF16)
    pairs = [(i, j) for i in range(1, nsub) for j in range(i)]
    if pairs:
        lhs = jnp.concatenate([q3[i] * jnp.exp2(c3[i] - rrep[j]) for i, j in pairs], axis=0).astype(BF16)
    col = lax.broadcasted_iota(jnp.int32, (HG_SUB, t), 1)
    colblk = col // HG_SUB
    row1 = lax.broadcasted_iota(jnp.int32, (HG_SUB, 1), 0)
    diag = col - row1

    for h in range(HG_HEADS):
        hs = slice(h * k, (h + 1) * k)
        if pairs:
            blk = _dot_nt(lhs[:, hs], kl[:, hs])
        for i in range(nsub):
            acc = jnp.zeros((HG_SUB, t), F32)
            for p, (pi, pj) in enumerate(pairs):
                if pi == i:
                    acc = jnp.where(colblk == pj, blk[p * HG_SUB:(p + 1) * HG_SUB, :], acc)
            att_s[h, i * HG_SUB:(i + 1) * HG_SUB, :] = acc
    yield 150

    for i in range(nsub):
        sub = slice(i * HG_SUB, (i + 1) * HG_SUB)
        qi, ci, ki = q[sub], cum[sub], kk[sub]
        accs = [att_s[h, sub, :] for h in range(HG_HEADS)]
        for d in range(HG_SUB):
            if d:
                p = qi * jnp.exp2(jnp.minimum(ci - pltpu.roll(ci, d, 0), 0.0)) * pltpu.roll(ki, d, 0)
            else:
                p = qi * ki
            hit = jnp.where(row1 >= d, diag, -2 * t) == i * HG_SUB - d
            for h in range(HG_HEADS):
                a = jnp.sum(p[:, h * k:(h + 1) * k], axis=-1, keepdims=True)
                accs[h] = jnp.where(hit, a, accs[h])
        for h in range(HG_HEADS):
            att_s[h, sub, :] = accs[h]
        if i % 2 == 1:
            yield 80

    iv = i_ref[rows, :].astype(BF16)
    for h in range(HG_HEADS):
        hs = slice(h * k, (h + 1) * k)
        sh = st[h]
        o = _dot(att_s[h].astype(BF16), iv[:, hs]) + _dot_nt(qe[:, hs], sh.astype(BF16))
        st[h] = sh * elast[:, hs] + _dot_tn(iv[:, hs], ke[:, hs])
        yc_ref[0, rows, hs] = (_rms(o, nw) * _silu(g_ref[rows, hs])).astype(yc_ref.dtype)
        if h % 2 == 1:
            yield 150


_MIX_COLS = {}
_off = 0
for _name, _width in (("z", SSD_INNER), ("xbc", SSD_CONV_DIM), ("q", HG_WIDTH), ("f", HG_WIDTH), ("i", HG_WIDTH),
                      ("g", HG_WIDTH), ("gates", 3 * D_MODEL), ("dt", LANES)):
    _MIX_COLS[_name] = (_off, _width)
    _off += _width
_MIX_WIDTH = _off


def _mix_kernel(x_ref, nw_ref, w_ref, h0_ref, cache_ref, s0_ref,
                cw_ref, cb_ref, dtb_ref, alog_ref, dsk_ref, snw_ref, lb_ref, hnw_ref,
                yb_ref, yc_ref, gates_ref, hn_ref, buf_ref, sn_ref,
                cbuf, xact, ht, ybuf, xwbuf, st, att_s, pz, pdt, pq, pf, pi, pg, hnb):
    rows = x_ref.shape[1]
    c = pl.program_id(1)
    gs = SSD_HPG * SSD_HEAD_DIM
    k = HG_EXPAND

    @pl.when(c == 0)
    def _():
        cbuf[0:HALO, :] = cache_ref[0]
        for g in range(SSD_GROUPS):
            ht[:, g * gs:(g + 1) * gs] = h0_ref[0, g * gs:(g + 1) * gs, :].T
        for h in range(HG_HEADS):
            st[h] = s0_ref[0, h * k:(h + 1) * k, :].T

    hnb[...] = _rms(x_ref[0], nw_ref[...]).astype(BF16)

    def piece(name, lo, width, store):
        off, _ = _MIX_COLS[name]
        return lambda: store(_dot(hnb[...], w_ref[:, off + lo:off + lo + width]), lo, width)

    def into(ref, row0=0):
        def store(v, lo, width):
            ref[row0:row0 + rows, lo:lo + width] = v.astype(ref.dtype)
        return store

    def into_gates(v, lo, width):
        gates_ref[0, :, lo:lo + width] = v.astype(gates_ref.dtype)

    pieces = [piece("xbc", lo, PROJ_COL, into(cbuf, HALO)) for lo in range(0, SSD_CONV_DIM, PROJ_COL)]
    pieces.insert(1, piece("dt", 0, LANES, into(pdt)))
    pieces += [piece("z", lo, PROJ_COL, into(pz)) for lo in range(0, SSD_INNER, PROJ_COL)]
    pieces += [piece(name, lo, PROJ_COL, into(ref)) for name, ref in (("q", pq), ("f", pf), ("i", pi), ("g", pg))
               for lo in range(0, HG_WIDTH, PROJ_COL)]
    pieces += [piece("gates", lo, PROJ_COL, into_gates) for lo in range(0, 3 * D_MODEL, PROJ_COL)]

    def issue(n=1):
        for _ in range(min(n, len(pieces))):
            pieces.pop(0)()

    issue(3)
    ts = min(rows, MIX_CHUNK)
    th = min(rows, HG_CHUNK)
    stages = [_ssd_conv_stages(rows, cw_ref, cb_ref, buf_ref, cbuf, xact)]
    stages += [_ssd_stages(slice(r0, r0 + ts), pz, pdt, dtb_ref, alog_ref, dsk_ref, snw_ref, yb_ref,
                           xact, ht, ybuf, xwbuf) for r0 in range(0, rows, ts)]
    stages += [_hgrn_stages(slice(r0, r0 + th), pq, pf, pi, pg, lb_ref[...], hnw_ref[...], yc_ref, st, att_s)
               for r0 in range(0, rows, th)]
    piece_cost = PIECE_COST * rows // MIX_CHUNK
    credit = 0
    for gen in stages:
        for weight in gen:
            credit += weight
            while credit >= piece_cost and pieces:
                issue()
                credit -= piece_cost
    issue(len(pieces))

    @pl.when(c == pl.num_programs(1) - 1)
    def _():
        for g in range(SSD_GROUPS):
            hn_ref[0, g * gs:(g + 1) * gs, :] = ht[:, g * gs:(g + 1) * gs].T
        for h in range(HG_HEADS):
            sn_ref[0, h * k:(h + 1) * k, :] = st[h].T


def _mix(x, norm_w, w_mix, ssd_h0, ssd_cache, hg_s0, ssd, hg_lb, hg_norm_w):
    b, n, _ = x.shape
    t = min(n, MIX_ROWS)
    ts = min(t, MIX_CHUNK)
    th = min(t, HG_CHUNK)
    srows = SSD_HEADS * SSD_HEAD_DIM
    hrows = HG_HEADS * HG_EXPAND
    tok = lambda w: pl.BlockSpec((1, t, w), lambda i, c: (i, c, 0))
    per_seq = lambda r, w: pl.BlockSpec((1, r, w), lambda i, c: (i, 0, 0))
    return pl.pallas_call(
        _mix_kernel,
        grid=(b, n // t),
        in_specs=[tok(D_MODEL), _resident((1, D_MODEL)), w_mix[1],
                  per_seq(srows, SSD_STATE), per_seq(HALO, SSD_CONV_DIM), per_seq(hrows, HG_EXPAND),
                  _resident((SSD_CONV, SSD_CONV_DIM)), _resident((1, SSD_CONV_DIM)),
                  _resident((1, LANES)), _resident((1, LANES)),
                  _resident((1, SSD_INNER)), _resident((1, SSD_INNER)),
                  _resident((1, HG_WIDTH)), _resident((1, HG_EXPAND))],
        out_specs=(tok(SSD_INNER), tok(HG_WIDTH), tok(3 * D_MODEL),
                   per_seq(srows, SSD_STATE), per_seq(HALO, SSD_CONV_DIM), per_seq(hrows, HG_EXPAND)),
        out_shape=(jax.ShapeDtypeStruct((b, n, SSD_INNER), BF16),
                   jax.ShapeDtypeStruct((b, n, HG_WIDTH), BF16),
                   jax.ShapeDtypeStruct((b, n, 3 * D_MODEL), BF16),
                   jax.ShapeDtypeStruct((b, srows, SSD_STATE), F32),
                   jax.ShapeDtypeStruct((b, HALO, SSD_CONV_DIM), F32),
                   jax.ShapeDtypeStruct((b, hrows, HG_EXPAND), F32)),
        scratch_shapes=[pltpu.VMEM((t + HALO, SSD_CONV_DIM), F32),
                        pltpu.VMEM((t, SSD_CONV_DIM), F32),
                        pltpu.VMEM((SSD_STATE, srows), F32),
                        pltpu.VMEM((ts, SSD_INNER), F32),
                        pltpu.VMEM((ts, SSD_INNER), BF16),
                        pltpu.VMEM((HG_HEADS, HG_EXPAND, HG_EXPAND), F32),
                        pltpu.VMEM((HG_HEADS, th, th), F32),
                        pltpu.VMEM((t, SSD_INNER), F32),
                        pltpu.VMEM((t, LANES), F32),
                        pltpu.VMEM((t, HG_WIDTH), F32), pltpu.VMEM((t, HG_WIDTH), F32),
                        pltpu.VMEM((t, HG_WIDTH), F32), pltpu.VMEM((t, HG_WIDTH), F32),
                        pltpu.VMEM((t, D_MODEL), BF16)],
        compiler_params=_cparams("parallel", "arbitrary"),
        name="mix",
    )(x, norm_w, w_mix[0], ssd_h0, ssd_cache, hg_s0, ssd["conv_w"], ssd["conv_b"], ssd["dt_bias"], ssd["a_log"],
      ssd["d"], ssd["norm_w"], hg_lb, hg_norm_w)


def _ffn_kernel(x_ref, cache_ref, nw_ref, wup_ref, cw_ref, cb_ref, wdn_ref, fw_ref,
                o_ref, buf_ref, abuf, gbuf, *, final_norm):
    nb, tt, _ = x_ref.shape

    @pl.when(pl.program_id(1) == 0)
    def _():
        abuf[:, 0:HALO, :] = cache_ref[...]

    x = x_ref[...].reshape(nb * tt, D_MODEL)
    hn = _rms(x, nw_ref[...]).astype(BF16)
    for c0 in range(0, FFN_DIM, FFN_COL):
        cs = slice(c0, c0 + FFN_COL)
        a = _dot(hn, wup_ref[:, cs])
        v = _dot(hn, wup_ref[:, FFN_DIM + c0:FFN_DIM + c0 + FFN_COL])
        abuf[:, HALO:HALO + tt, cs] = a.reshape(nb, tt, FFN_COL)
        conv = cb_ref[:, cs] + cw_ref[FFN_CONV - 1:FFN_CONV, cs] * a.reshape(nb, tt, FFN_COL)
        for j in range(FFN_CONV - 1):
            off = HALO - (FFN_CONV - 1) + j
            conv = conv + cw_ref[j:j + 1, cs] * abuf[:, off:off + tt, cs]
        gbuf[:, cs] = (_silu(conv).reshape(nb * tt, FFN_COL) * v).astype(BF16)
    tail = abuf[:, tt:tt + HALO, :]
    abuf[:, 0:HALO, :] = tail
    buf_ref[...] = tail
    y = x + _dot(gbuf[...], wdn_ref[...])
    if final_norm:
        y = _rms(y, fw_ref[...])
    o_ref[...] = y.reshape(nb, tt, D_MODEL)


def _ffn(x, cache, norm_w, w_up, conv_w, conv_b, w_down, final_w, final_norm):
    b, n, _ = x.shape
    tt = min(n, ROW_TILE)
    nb = ROW_TILE // tt
    return pl.pallas_call(
        functools.partial(_ffn_kernel, final_norm=final_norm),
        grid=(b // nb, n // tt),
        in_specs=[pl.BlockSpec((nb, tt, D_MODEL), lambda i, c: (i, c, 0)),
                  pl.BlockSpec((nb, HALO, FFN_DIM), lambda i, c: (i, 0, 0)),
                  _resident((1, D_MODEL)), w_up[1],
                  _resident((FFN_CONV, FFN_DIM)), _resident((1, FFN_DIM)),
                  w_down[1], _resident((1, D_MODEL))],
        out_specs=(pl.BlockSpec((nb, tt, D_MODEL), lambda i, c: (i, c, 0)),
                   pl.BlockSpec((nb, HALO, FFN_DIM), lambda i, c: (i, 0, 0))),
        out_shape=(jax.ShapeDtypeStruct((b, n, D_MODEL), F32),
                   jax.ShapeDtypeStruct((b, HALO, FFN_DIM), F32)),
        scratch_shapes=[pltpu.VMEM((nb, tt + HALO, FFN_DIM), F32),
                        pltpu.VMEM((nb * tt, FFN_DIM), BF16)],
        compiler_params=_cparams("parallel", "arbitrary"),
        name="ffn",
    )(x, cache, norm_w, w_up[0], conv_w, conv_b, w_down[0], final_w)


def _pad_cache(cache):
    return jnp.pad(cache, ((0, 0), (HALO - cache.shape[1], 0), (0, 0)))


def _split_w_in(w_in):
    sizes = (S5_WIDTH, SSD_INNER, SSD_CONV_DIM, SSD_HEADS, HG_WIDTH, HG_WIDTH, HG_WIDTH, HG_WIDTH, 3 * D_MODEL)
    offs = [0]
    for s in sizes:
        offs.append(offs[-1] + s)
    u, z, xbc, dt, q, f, i, g, gates = (w_in[..., offs[k]:offs[k + 1]] for k in range(len(sizes)))
    dt = jnp.pad(dt, ((0, 0),) * (dt.ndim - 1) + ((0, LANES - SSD_HEADS),))
    return u.astype(BF16), jnp.concatenate([z, xbc, q, f, i, g, gates, dt], axis=-1).astype(BF16)


def _run_trunk(x, s5_re, s5_im, ssd_h, ssd_buf, hg_s, ffn_buf, layers, final_w):
    b, n, _ = x.shape
    outs = ([], [], [], [], [], [])
    for l, w in enumerate(layers):
        yb, yc, gates, hb, bufb, sc = _mix(x, w["norm_mix_w"], w["w_mix"], ssd_h[l].reshape(b, -1, SSD_STATE),
                                           _pad_cache(ssd_buf[l]), hg_s[l].reshape(b, -1, HG_EXPAND),
                                           w["ssd"], w["hg_lb"], w["hg_norm_w"])
        h, hr, hi = _s5_merge(x, yb, yc, gates, w["norm_mix_w"], w["w_u"], s5_re[l].reshape(b, -1),
                              s5_im[l].reshape(b, -1), w["s5"], w["w_branch_a"], w["w_branch_b"], w["w_branch_c"],
                              w["w_out"])
        x, fb = _ffn(h, _pad_cache(ffn_buf[l]), w["norm_ffn_w"], w["ffn_w_up"],
                     w["ffn_conv_w"], w["ffn_conv_b"], w["ffn_w_down"], final_w, l == len(layers) - 1)
        vals = (hr.reshape(b, S5_GROUPS, S5_STATE), hi.reshape(b, S5_GROUPS, S5_STATE),
                hb.reshape(b, SSD_HEADS, SSD_HEAD_DIM, SSD_STATE), bufb[:, HALO - (SSD_CONV - 1):],
                sc.reshape(b, HG_HEADS, HG_EXPAND, HG_EXPAND), fb[:, HALO - (FFN_CONV - 1):])
        for lst, val in zip(outs, vals):
            lst.append(val)
    return (x,) + tuple(jnp.stack(v) for v in outs)


def kernel(x_prompt, x_sample, state_s5_re, state_s5_im, state_ssd, cache_ssd_conv, state_hgrn, cache_ffn_conv,
           norm_mix_w, w_in, s5_lambda_re, s5_lambda_im, s5_log_dt, s5_b_re, s5_b_im, s5_c_re, s5_c_im,
           s5_d, s5_glu_w, s5_glu_b, ssd_conv_w, ssd_conv_b, ssd_dt_bias, ssd_a_log, ssd_d, ssd_norm_w,
           hg_lb_logits, hg_norm_w, w_branch_a, w_branch_b, w_branch_c, w_out,
           norm_ffn_w, ffn_w_up, ffn_conv_w, ffn_conv_b, ffn_w_down, norm_final_w):
    lb_cum = jnp.cumsum(jax.nn.softmax(hg_lb_logits.astype(F32), axis=0), axis=0)
    hg_lb = lb_cum - lb_cum[0]
    w_u, w_mix = _split_w_in(w_in)
    glu_w, wa, wb, wc, wo, w_up, w_down = (w.astype(BF16) for w in (
        s5_glu_w, w_branch_a, w_branch_b, w_branch_c, w_out, ffn_w_up, ffn_w_down))
    layers = []
    for l in range(DEPTH):
        layers.append(dict(
            norm_mix_w=norm_mix_w[l].reshape(1, -1), w_u=_layer(w_u, l), w_mix=_layer(w_mix, l),
            s5=_s5_params(s5_lambda_re[l], s5_lambda_im[l], s5_log_dt[l], s5_b_re[l], s5_b_im[l],
                          s5_c_re[l], s5_c_im[l], s5_d[l], _layer(glu_w, l), s5_glu_b[l]),
            ssd=_ssd_params(ssd_conv_w[l], ssd_conv_b[l], ssd_dt_bias[l], ssd_a_log[l], ssd_d[l], ssd_norm_w[l]),
            hg_lb=hg_lb[l].reshape(1, -1), hg_norm_w=hg_norm_w[l].reshape(1, -1),
            w_branch_a=_layer(wa, l), w_branch_b=_layer(wb, l), w_branch_c=_layer(wc, l), w_out=_layer(wo, l),
            norm_ffn_w=norm_ffn_w[l].reshape(1, -1), ffn_w_up=_layer(w_up, l),
            ffn_conv_w=ffn_conv_w[l], ffn_conv_b=ffn_conv_b[l].reshape(1, -1),
            ffn_w_down=_layer(w_down, l)))
    final_w = norm_final_w.reshape(1, -1)
    bp = x_prompt.shape[0]
    zeros = lambda *s: jnp.zeros((DEPTH, bp) + s, F32)
    p = _run_trunk(x_prompt, zeros(S5_GROUPS, S5_STATE), zeros(S5_GROUPS, S5_STATE),
                   zeros(SSD_HEADS, SSD_HEAD_DIM, SSD_STATE), zeros(SSD_CONV - 1, SSD_CONV_DIM),
                   zeros(HG_HEADS, HG_EXPAND, HG_EXPAND), zeros(FFN_CONV - 1, FFN_DIM), layers, final_w)
    s = _run_trunk(x_sample, state_s5_re, state_s5_im, state_ssd, cache_ssd_conv, state_hgrn, cache_ffn_conv,
                   layers, final_w)
    return (p[0], s[0]) + p[1:] + s[1:]
```

```python
import functools

import jax
import jax.numpy as jnp
from jax import lax
from jax.experimental import pallas as pl
from jax.experimental.pallas import tpu as pltpu

F32 = jnp.float32
BF16 = jnp.bfloat16
EPS = 1e-6

D_MODEL = 1024
DEPTH = 2
S5_WIDTH = 512
S5_GROUP = 16
S5_GROUPS = 32
S5_STATE = 64
S5_SLABS = 4
SSD_INNER = 1024
SSD_HEAD_DIM = 64
SSD_HEADS = 16
SSD_GROUPS = 4
SSD_HPG = 4
SSD_STATE = 128
SSD_CONV = 4
SSD_CONV_DIM = 2048
HG_WIDTH = 512
HG_EXPAND = 128
HG_HEADS = 4
HG_CHUNK = 64
HG_SUB = 8
LOG2E = 1.4426950408889634
MIX_CHUNK = 128
MIX_ROWS = 256
PROJ_COL = 256
PIECE_COST = PROJ_COL * 54 // 64
CONV_COST = PROJ_COL * 25 // 32
FFN_DIM = 2816
FFN_CONV = 3
FFN_COL = 256
LANES = 128
HALO = 8
ROW_TILE = 512
VMEM_LIMIT = 56 * 1024 * 1024


def _cparams(*sem):
    return pltpu.CompilerParams(dimension_semantics=sem, vmem_limit_bytes=VMEM_LIMIT)


def _resident(shape):
    n = len(shape)
    return pl.BlockSpec(shape, lambda *_: (0,) * n, pipeline_mode=pl.Buffered(1))


def _layer(stacked, l):
    n = stacked.ndim - 1
    spec = pl.BlockSpec((None,) + stacked.shape[1:], lambda *_: (l,) + (0,) * n, pipeline_mode=pl.Buffered(1))
    return stacked, spec


def _rms(x, w):
    return x * lax.rsqrt(jnp.mean(x * x, axis=-1, keepdims=True) + EPS) * w


def _silu(x):
    return x * jax.nn.sigmoid(x)


def _dot(a, b):
    return jnp.dot(a, b, preferred_element_type=F32)


def _dot_nt(a, b):
    return lax.dot_general(a, b, (((1,), (1,)), ((), ())), preferred_element_type=F32)


def _dot_tn(a, b):
    return lax.dot_general(a, b, (((0,), (0,)), ((), ())), preferred_element_type=F32)


def _cumsum_rows(tri, x):
    return jnp.dot(tri, x, preferred_element_type=F32, precision=lax.Precision.HIGHEST)


def _tri(t):
    r = lax.broadcasted_iota(jnp.int32, (t, t), 0)
    c = lax.broadcasted_iota(jnp.int32, (t, t), 1)
    return r >= c


def _gelu_tanh(x):
    c = 0.7978845608028654
    return 0.5 * x * (1.0 + jnp.tanh(c * (x + 0.044715 * (x * x * x))))


def _s5_kernel(x_ref, yb_ref, yc_ref, gl_ref, nw_ref, wu_ref, h0r_ref, h0i_ref, ar_ref, ai_ref, bbd_ref, cbd_ref,
               d_ref, gw_ref, gb_ref, perm_ref, wa_ref, wb_ref, wc_ref, wo_ref,
               o_ref, hr_ref, hi_ref, ubm, utm, xs0, xs1, ytm, st, macc):
    nb, tt, _ = x_ref.shape
    half = S5_STATE * S5_GROUPS // S5_SLABS
    xs_bufs = (xs0, xs1)
    m = nb * tt
    d = D_MODEL
    mq = d // S5_SLABS

    def gate(k, cols):
        return jax.nn.sigmoid(gl_ref[:, :, k * d + cols.start:k * d + cols.stop].reshape(m, mq).astype(F32))

    @pl.when(pl.program_id(0) == 0)
    def _():
        for j in range(S5_SLABS):
            st[2 * j] = h0r_ref[:, j * half:(j + 1) * half]
            st[2 * j + 1] = h0i_ref[:, j * half:(j + 1) * half]

    hn = _rms(x_ref[...].reshape(nb * tt, D_MODEL), nw_ref[...]).astype(BF16)
    ubm[...] = _dot(hn, wu_ref[...]).reshape(nb, tt, S5_WIDTH)
    for t in range(tt):
        utm[t] = ubm[:, t, :]

    def project(j):
        u_slab = utm[:, :, j * LANES:(j + 1) * LANES].reshape(tt * nb, LANES)
        xs_bufs[j % 2][...] = _dot(u_slab.astype(BF16), bbd_ref[j])

    project(0)
    for j in range(S5_SLABS):
        xs = xs_bufs[j % 2]
        if j + 1 < S5_SLABS:
            project(j + 1)
        cols = slice(j * mq, (j + 1) * mq)
        macc[:, cols] = (gate(1, cols) * _dot(yb_ref[...].reshape(m, SSD_INNER), wb_ref[:, cols])
                         + gate(2, cols) * _dot(yc_ref[...].reshape(m, HG_WIDTH), wc_ref[:, cols]))
        ar = jnp.broadcast_to(ar_ref[:, j * half:(j + 1) * half], (nb, half))
        ai = jnp.broadcast_to(ai_ref[:, j * half:(j + 1) * half], (nb, half))
        sr, si = st[2 * j], st[2 * j + 1]
        for t in range(tt):
            rows = slice(t * nb, (t + 1) * nb)
            sr, si = (ar * sr - ai * si + xs[rows, 0:half], ar * si + ai * sr + xs[rows, half:2 * half])
            xs[rows, 0:half] = sr
            xs[rows, half:2 * half] = si
        st[2 * j] = sr
        st[2 * j + 1] = si
        hr_ref[:, j * half:(j + 1) * half] = sr
        hi_ref[:, j * half:(j + 1) * half] = si
        ytm[:, j * LANES:(j + 1) * LANES] = _dot(xs[...].astype(BF16), cbd_ref[j])

    u2 = utm[...].reshape(tt * nb, S5_WIDTH)

    y = ytm[...] + d_ref[...] * u2
    a = _gelu_tanh(y)
    out = a * jax.nn.sigmoid(_dot(a.astype(BF16), gw_ref[...]) + gb_ref[...])
    ya = _dot(perm_ref[...], out.astype(BF16)).astype(BF16)
    for j in range(S5_SLABS):
        cols = slice(j * mq, (j + 1) * mq)
        macc[:, cols] += gate(0, cols) * _dot(ya, wa_ref[:, cols])
    h = x_ref[...].reshape(m, d) + _dot(macc[...].astype(BF16), wo_ref[...])
    o_ref[...] = h.reshape(nb, tt, d)


def _time_to_batch_major(tt, nb):
    dst = jnp.arange(tt * nb)
    src = (dst % tt) * nb + dst // tt
    return (src[:, None] == jnp.arange(tt * nb)[None, :]).astype(BF16)


def _s5_merge(x, yb, yc, gates, norm_w, w_u, h0r, h0i, prm, wa, wb, wc, wo):
    b, n, _ = x.shape
    tt = ROW_TILE // b
    nst = S5_GROUPS * S5_STATE
    half = nst // S5_SLABS
    tok = lambda w: pl.BlockSpec((b, tt, w), lambda t: (0, t, 0))
    return pl.pallas_call(
        _s5_kernel,
        grid=(n // tt,),
        in_specs=[tok(D_MODEL), tok(SSD_INNER), tok(HG_WIDTH), tok(3 * D_MODEL),
                  _resident((1, D_MODEL)), w_u[1],
                  _resident((b, nst)), _resident((b, nst)),
                  _resident((1, nst)), _resident((1, nst)),
                  _resident((S5_SLABS, LANES, 2 * half)), _resident((S5_SLABS, 2 * half, LANES)),
                  _resident((1, S5_WIDTH)), prm["glu_w"][1], _resident((1, S5_WIDTH)),
                  _resident((tt * b, tt * b)), wa[1], wb[1], wc[1], wo[1]],
        out_specs=(tok(D_MODEL),
                   pl.BlockSpec((b, nst), lambda t: (0, 0)),
                   pl.BlockSpec((b, nst), lambda t: (0, 0))),
        out_shape=(jax.ShapeDtypeStruct((b, n, D_MODEL), F32),
                   jax.ShapeDtypeStruct((b, nst), F32),
                   jax.ShapeDtypeStruct((b, nst), F32)),
        scratch_shapes=[pltpu.VMEM((b, tt, S5_WIDTH), F32),
                        pltpu.VMEM((tt, b, S5_WIDTH), F32),
                        pltpu.VMEM((tt * b, 2 * half), F32), pltpu.VMEM((tt * b, 2 * half), F32),
                        pltpu.VMEM((tt * b, S5_WIDTH), F32),
                        pltpu.VMEM((2 * S5_SLABS, b, half), F32),
                        pltpu.VMEM((tt * b, D_MODEL), F32)],
        compiler_params=_cparams("arbitrary"),
        name="s5_merge",
    )(x, yb, yc, gates, norm_w, w_u[0], h0r, h0i, prm["ar"], prm["ai"], prm["bbd"], prm["cbd"], prm["d"],
      prm["glu_w"][0], prm["glu_b"], _time_to_batch_major(tt, b), wa[0], wb[0], wc[0], wo[0])


def _s5_params(lam_re, lam_im, log_dt, b_re, b_im, c_re, c_im, d_skip, glu_w, glu_b):
    dt = jnp.exp(log_dt)[:, None]
    mag = jnp.exp(lam_re * dt)
    ar, ai = mag * jnp.cos(lam_im * dt), mag * jnp.sin(lam_im * dt)
    den = lam_re * lam_re + lam_im * lam_im
    nr, ni = ar - 1.0, ai
    cr = (nr * lam_re + ni * lam_im) / den
    ci = (ni * lam_re - nr * lam_im) / den
    bbr = cr[..., None] * b_re - ci[..., None] * b_im
    bbi = cr[..., None] * b_im + ci[..., None] * b_re
    gps = S5_GROUPS // S5_SLABS
    eye = jnp.eye(gps, dtype=F32)

    def pack_b(bb):
        bb = bb.reshape(S5_SLABS, gps, S5_STATE, S5_GROUP)
        return jnp.einsum("sgph,gk->sghkp", bb, eye).reshape(S5_SLABS, gps * S5_GROUP, gps * S5_STATE)

    def pack_c(cc):
        cc = cc.reshape(S5_SLABS, gps, S5_GROUP, S5_STATE)
        return jnp.einsum("sghp,gk->sgpkh", cc, eye).reshape(S5_SLABS, gps * S5_STATE, gps * S5_GROUP)

    bbd = jnp.concatenate([pack_b(bbr), pack_b(bbi)], axis=2).astype(BF16)
    cbd = jnp.concatenate([pack_c(c_re), -pack_c(c_im)], axis=1).astype(BF16)
    return dict(ar=ar.reshape(1, -1), ai=ai.reshape(1, -1), bbd=bbd, cbd=cbd,
                d=d_skip.reshape(1, -1), glu_w=glu_w, glu_b=glu_b.reshape(1, -1))


def _ssd_conv_stages(rows, cw_ref, cb_ref, buf_ref, cbuf, xact):
    for c0 in range(0, SSD_CONV_DIM, PROJ_COL):
        yield "xbc%d" % c0
        cs = slice(c0, c0 + PROJ_COL)
        acc = cb_ref[:, cs] + cw_ref[SSD_CONV - 1:SSD_CONV, cs] * cbuf[HALO:HALO + rows, cs]
        for j in range(SSD_CONV - 1):
            off = HALO - (SSD_CONV - 1) + j
            acc = acc + cw_ref[j:j + 1, cs] * cbuf[off:off + rows, cs]
        xact[:, cs] = _silu(acc)
        tail = cbuf[rows:rows + HALO, cs]
        cbuf[0:HALO, cs] = tail
        buf_ref[0, :, cs] = tail
        yield CONV_COST * rows // MIX_CHUNK


def _ssd_stages(rows, z_ref, dt_ref, dtb_ref, alog_ref, dsk_ref, nw_ref, yb_ref, xact_ref, ht, ybuf, xwbuf):
    t = rows.stop - rows.start
    gs = SSD_HPG * SSD_HEAD_DIM
    xact = xact_ref.at[rows]
    yield "dt"
    dt = jax.nn.softplus(dt_ref[rows, :] + dtb_ref[...])
    la = dt * (-LOG2E * jnp.exp(alog_ref[...]))
    mask = _tri(t)
    cum = _cumsum_rows(mask.astype(F32), la)
    cum_t = cum.T
    last = cum[t - 1:t, :]
    ecum = jnp.exp2(cum)
    wgt = jnp.exp2(last - cum) * dt
    elast = jnp.exp2(last)
    yield 150

    b0 = SSD_INNER
    c0 = SSD_INNER + SSD_GROUPS * SSD_STATE
    for g in range(SSD_GROUPS):
        bg = xact[:, b0 + g * SSD_STATE:b0 + (g + 1) * SSD_STATE].astype(BF16)
        cg = xact[:, c0 + g * SSD_STATE:c0 + (g + 1) * SSD_STATE].astype(BF16)
        scores = _dot_nt(cg, bg)
        hg = ht[:, g * gs:(g + 1) * gs]
        yint = _dot(cg, hg.astype(BF16))
        for r in range(SSD_HPG):
            h = g * SSD_HPG + r
            hs = slice(h * SSD_HEAD_DIM, (h + 1) * SSD_HEAD_DIM)
            xh = xact[:, hs]
            seg = cum[:, h:h + 1] - cum_t[h:h + 1, :]
            decay = jnp.exp2(jnp.where(mask, seg, -jnp.inf))
            m = (scores * decay).astype(BF16)
            y = _dot(m, (dt[:, h:h + 1] * xh).astype(BF16))
            y = y + yint[:, r * SSD_HEAD_DIM:(r + 1) * SSD_HEAD_DIM] * ecum[:, h:h + 1]
            ybuf[:, hs] = y
            xwbuf[:, hs] = (wgt[:, h:h + 1] * xh).astype(BF16)
            ht[:, hs] = hg[:, r * SSD_HEAD_DIM:(r + 1) * SSD_HEAD_DIM] * elast[:, h:h + 1]
        ht[:, g * gs:(g + 1) * gs] += _dot_tn(bg, xwbuf[:, g * gs:(g + 1) * gs])
        yield 350

    yield "z"
    y = ybuf[...] + dsk_ref[...] * xact[:, :SSD_INNER]
    yb_ref[0, rows, :] = _rms(y * _silu(z_ref[rows, :]), nw_ref[...]).astype(yb_ref.dtype)
    yield 300


def _pad_lanes(v):
    return jnp.pad(v, (0, LANES - v.shape[0])).reshape(1, LANES)


def _ssd_params(conv_w, conv_b, dt_bias, a_log, d_skip, norm_w):
    return dict(conv_w=conv_w, conv_b=conv_b.reshape(1, -1), dt_bias=_pad_lanes(dt_bias), a_log=_pad_lanes(a_log),
                d=jnp.repeat(d_skip, SSD_HEAD_DIM).reshape(1, -1), norm_w=norm_w.reshape(1, -1))


def _hgrn_stages(rows, q_ref, f_ref, i_ref, g_ref, lb, nw, yc_ref, st, att_s):
    t = rows.stop - rows.start
    nsub = t // HG_SUB
    k = HG_EXPAND
    yield "qf"
    q = q_ref[rows, :]
    fz = f_ref[rows, :]
    e = jnp.exp(-jnp.abs(fz))
    lsp = jnp.minimum(fz, 0.0) - jnp.log(1.0 + e)
    lsn = lsp - fz
    lbt = jnp.log(lb) + lsn
    log_f = jnp.maximum(lsp, lbt) + jnp.log(1.0 + jnp.exp(-jnp.abs(lsp - lbt)))
    kk = (1.0 - lb) * (jnp.where(fz >= 0.0, e, 1.0) / (1.0 + e))

    cum = _cumsum_rows(_tri(t).astype(F32), log_f * LOG2E)
    last = cum[t - 1:t, :]
    qe = (q * jnp.exp2(cum)).astype(BF16)
    ke = (kk * jnp.exp2(last - cum)).astype(BF16)
    elast = jnp.exp2(last)
    yield 350

    c3 = cum.reshape(nsub, HG_SUB, HG_WIDTH)
    q3 = q.reshape(nsub, HG_SUB, HG_WIDTH)
    rrep = jnp.broadcast_to(c3[:, HG_SUB - 1:HG_SUB, :], c3.shape)
    kl = (kk * jnp.exp2((rrep - c3).reshape(t, HG_WIDTH))).astype(BF16)
    pairs = [(i, j) for i in range(1, nsub) for j in range(i)]
    if pairs:
        lhs = jnp.concatenate([q3[i] * jnp.exp2(c3[i] - rrep[j]) for i, j in pairs], axis=0).astype(BF16)
    col = lax.broadcasted_iota(jnp.int32, (HG_SUB, t), 1)
    colblk = col // HG_SUB
    row1 = lax.broadcasted_iota(jnp.int32, (HG_SUB, 1), 0)
    diag = col - row1

    for h in range(HG_HEADS):
        hs = slice(h * k, (h + 1) * k)
        if pairs:
            blk = _dot_nt(lhs[:, hs], kl[:, hs])
        for i in range(nsub):
            acc = jnp.zeros((HG_SUB, t), F32)
            for p, (pi, pj) in enumerate(pairs):
                if pi == i:
                    acc = jnp.where(colblk == pj, blk[p * HG_SUB:(p + 1) * HG_SUB, :], acc)
            att_s[h, i * HG_SUB:(i + 1) * HG_SUB, :] = acc
    yield 150

    for i in range(nsub):
        sub = slice(i * HG_SUB, (i + 1) * HG_SUB)
        qi, ci, ki = q[sub], cum[sub], kk[sub]
        accs = [att_s[h, sub, :] for h in range(HG_HEADS)]
        for d in range(HG_SUB):
            if d:
                p = qi * jnp.exp2(jnp.minimum(ci - pltpu.roll(ci, d, 0), 0.0)) * pltpu.roll(ki, d, 0)
            else:
                p = qi * ki
            hit = jnp.where(row1 >= d, diag, -2 * t) == i * HG_SUB - d
            for h in range(HG_HEADS):
                a = jnp.sum(p[:, h * k:(h + 1) * k], axis=-1, keepdims=True)
                accs[h] = jnp.where(hit, a, accs[h])
        for h in range(HG_HEADS):
            att_s[h, sub, :] = accs[h]
        if i % 2 == 1:
            yield 80

    yield "ig"
    iv = i_ref[rows, :].astype(BF16)
    for h in range(HG_HEADS):
        hs = slice(h * k, (h + 1) * k)
        sh = st[h]
        o = _dot(att_s[h].astype(BF16), iv[:, hs]) + _dot_nt(qe[:, hs], sh.astype(BF16))
        st[h] = sh * elast[:, hs] + _dot_tn(iv[:, hs], ke[:, hs])
        yc_ref[0, rows, hs] = (_rms(o, nw) * _silu(g_ref[rows, hs])).astype(yc_ref.dtype)
        if h % 2 == 1:
            yield 150


_MIX_COLS = {}
_off = 0
for _name, _width in (("z", SSD_INNER), ("xbc", SSD_CONV_DIM), ("q", HG_WIDTH), ("f", HG_WIDTH), ("i", HG_WIDTH),
                      ("g", HG_WIDTH), ("gates", 3 * D_MODEL), ("dt", LANES)):
    _MIX_COLS[_name] = (_off, _width)
    _off += _width
_MIX_WIDTH = _off


def _mix_kernel(x_ref, nw_ref, w_ref, h0_ref, cache_ref, s0_ref,
                cw_ref, cb_ref, dtb_ref, alog_ref, dsk_ref, snw_ref, lb_ref, hnw_ref,
                yb_ref, yc_ref, gates_ref, hn_ref, buf_ref, sn_ref,
                cbuf, xact, ht, ybuf, xwbuf, st, att_s, pz, pdt, pq, pf, pi, pg, hnb):
    rows = x_ref.shape[1]
    c = pl.program_id(1)
    gs = SSD_HPG * SSD_HEAD_DIM
    k = HG_EXPAND

    @pl.when(c == 0)
    def _():
        cbuf[0:HALO, :] = cache_ref[0]
        for g in range(SSD_GROUPS):
            ht[:, g * gs:(g + 1) * gs] = h0_ref[0, g * gs:(g + 1) * gs, :].T
        for h in range(HG_HEADS):
            st[h] = s0_ref[0, h * k:(h + 1) * k, :].T

    hnb[...] = _rms(x_ref[0], nw_ref[...]).astype(BF16)

    def piece(name, lo, width, store):
        off, _ = _MIX_COLS[name]
        return lambda: store(_dot(hnb[...], w_ref[:, off + lo:off + lo + width]), lo, width)

    def into(ref, row0=0):
        def store(v, lo, width):
            ref[row0:row0 + rows, lo:lo + width] = v.astype(ref.dtype)
        return store

    def into_gates(v, lo, width):
        gates_ref[0, :, lo:lo + width] = v.astype(gates_ref.dtype)

    pieces = [("xbc%d" % lo, piece("xbc", lo, PROJ_COL, into(cbuf, HALO))) for lo in range(0, SSD_CONV_DIM, PROJ_COL)]
    pieces.insert(1, ("dt", piece("dt", 0, LANES, into(pdt))))
    for tag, group in (("qf", (("q", pq), ("f", pf))), ("ig", (("i", pi), ("g", pg)))):
        pieces += [(tag, piece(name, lo, PROJ_COL, into(ref))) for name, ref in group
                   for lo in range(0, HG_WIDTH, PROJ_COL)]
    pieces += [("z", piece("z", lo, PROJ_COL, into(pz))) for lo in range(0, SSD_INNER, PROJ_COL)]
    pieces += [("gates", piece("gates", lo, PROJ_COL, into_gates)) for lo in range(0, 3 * D_MODEL, PROJ_COL)]

    def issue(n=1):
        for _ in range(min(n, len(pieces))):
            pieces.pop(0)[1]()

    def chain(gens):
        for gen in gens:
            yield from gen

    ts = min(rows, MIX_CHUNK)
    th = min(rows, HG_CHUNK)
    piece_cost = PIECE_COST * rows // MIX_CHUNK
    credit = [0]

    def advance(gen):
        for item in gen:
            if isinstance(item, str):
                while any(tag == item for tag, _ in pieces):
                    issue()
                continue
            credit[0] += item
            while credit[0] >= piece_cost and pieces:
                issue()
                credit[0] -= piece_cost
            return True
        return False

    issue(3)
    conv = _ssd_conv_stages(rows, cw_ref, cb_ref, buf_ref, cbuf, xact)
    while advance(conv):
        pass
    live = [chain([_ssd_stages(slice(r0, r0 + ts), pz, pdt, dtb_ref, alog_ref, dsk_ref, snw_ref, yb_ref,
                               xact, ht, ybuf, xwbuf) for r0 in range(0, rows, ts)]),
            chain([_hgrn_stages(slice(r0, r0 + th), pq, pf, pi, pg, lb_ref[...], hnw_ref[...], yc_ref, st, att_s)
                   for r0 in range(0, rows, th)])]
    while live:
        live = [gen for gen in live if advance(gen)]
    issue(len(pieces))

    @pl.when(c == pl.num_programs(1) - 1)
    def _():
        for g in range(SSD_GROUPS):
            hn_ref[0, g * gs:(g + 1) * gs, :] = ht[:, g * gs:(g + 1) * gs].T
        for h in range(HG_HEADS):
            sn_ref[0, h * k:(h + 1) * k, :] = st[h].T


def _mix(x, norm_w, w_mix, ssd_h0, ssd_cache, hg_s0, ssd, hg_lb, hg_norm_w):
    b, n, _ = x.shape
    t = min(n, MIX_ROWS)
    ts = min(t, MIX_CHUNK)
    th = min(t, HG_CHUNK)
    srows = SSD_HEADS * SSD_HEAD_DIM
    hrows = HG_HEADS * HG_EXPAND
    tok = lambda w: pl.BlockSpec((1, t, w), lambda i, c: (i, c, 0))
    per_seq = lambda r, w: pl.BlockSpec((1, r, w), lambda i, c: (i, 0, 0))
    return pl.pallas_call(
        _mix_kernel,
        grid=(b, n // t),
        in_specs=[tok(D_MODEL), _resident((1, D_MODEL)), w_mix[1],
                  per_seq(srows, SSD_STATE), per_seq(HALO, SSD_CONV_DIM), per_seq(hrows, HG_EXPAND),
                  _resident((SSD_CONV, SSD_CONV_DIM)), _resident((1, SSD_CONV_DIM)),
                  _resident((1, LANES)), _resident((1, LANES)),
                  _resident((1, SSD_INNER)), _resident((1, SSD_INNER)),
                  _resident((1, HG_WIDTH)), _resident((1, HG_EXPAND))],
        out_specs=(tok(SSD_INNER), tok(HG_WIDTH), tok(3 * D_MODEL),
                   per_seq(srows, SSD_STATE), per_seq(HALO, SSD_CONV_DIM), per_seq(hrows, HG_EXPAND)),
        out_shape=(jax.ShapeDtypeStruct((b, n, SSD_INNER), BF16),
                   jax.ShapeDtypeStruct((b, n, HG_WIDTH), BF16),
                   jax.ShapeDtypeStruct((b, n, 3 * D_MODEL), BF16),
                   jax.ShapeDtypeStruct((b, srows, SSD_STATE), F32),
                   jax.ShapeDtypeStruct((b, HALO, SSD_CONV_DIM), F32),
                   jax.ShapeDtypeStruct((b, hrows, HG_EXPAND), F32)),
        scratch_shapes=[pltpu.VMEM((t + HALO, SSD_CONV_DIM), F32),
                        pltpu.VMEM((t, SSD_CONV_DIM), F32),
                        pltpu.VMEM((SSD_STATE, srows), F32),
                        pltpu.VMEM((ts, SSD_INNER), F32),
                        pltpu.VMEM((ts, SSD_INNER), BF16),
                        pltpu.VMEM((HG_HEADS, HG_EXPAND, HG_EXPAND), F32),
                        pltpu.VMEM((HG_HEADS, th, th), F32),
                        pltpu.VMEM((t, SSD_INNER), F32),
                        pltpu.VMEM((t, LANES), F32),
                        pltpu.VMEM((t, HG_WIDTH), F32), pltpu.VMEM((t, HG_WIDTH), F32),
                        pltpu.VMEM((t, HG_WIDTH), F32), pltpu.VMEM((t, HG_WIDTH), F32),
                        pltpu.VMEM((t, D_MODEL), BF16)],
        compiler_params=_cparams("parallel", "arbitrary"),
        name="mix",
    )(x, norm_w, w_mix[0], ssd_h0, ssd_cache, hg_s0, ssd["conv_w"], ssd["conv_b"], ssd["dt_bias"], ssd["a_log"],
      ssd["d"], ssd["norm_w"], hg_lb, hg_norm_w)


def _ffn_kernel(x_ref, cache_ref, nw_ref, wup_ref, cw_ref, cb_ref, wdn_ref, fw_ref,
                o_ref, buf_ref, abuf, gbuf, *, final_norm):
    nb, tt, _ = x_ref.shape

    @pl.when(pl.program_id(1) == 0)
    def _():
        abuf[:, 0:HALO, :] = cache_ref[...]

    x = x_ref[...].reshape(nb * tt, D_MODEL)
    hn = _rms(x, nw_ref[...]).astype(BF16)
    for c0 in range(0, FFN_DIM, FFN_COL):
        cs = slice(c0, c0 + FFN_COL)
        a = _dot(hn, wup_ref[:, cs])
        v = _dot(hn, wup_ref[:, FFN_DIM + c0:FFN_DIM + c0 + FFN_COL])
        abuf[:, HALO:HALO + tt, cs] = a.reshape(nb, tt, FFN_COL)
        conv = cb_ref[:, cs] + cw_ref[FFN_CONV - 1:FFN_CONV, cs] * a.reshape(nb, tt, FFN_COL)
        for j in range(FFN_CONV - 1):
            off = HALO - (FFN_CONV - 1) + j
            conv = conv + cw_ref[j:j + 1, cs] * abuf[:, off:off + tt, cs]
        gbuf[:, cs] = (_silu(conv).reshape(nb * tt, FFN_COL) * v).astype(BF16)
    tail = abuf[:, tt:tt + HALO, :]
    abuf[:, 0:HALO, :] = tail
    buf_ref[...] = tail
    y = x + _dot(gbuf[...], wdn_ref[...])
    if final_norm:
        y = _rms(y, fw_ref[...])
    o_ref[...] = y.reshape(nb, tt, D_MODEL)


def _ffn(x, cache, norm_w, w_up, conv_w, conv_b, w_down, final_w, final_norm):
    b, n, _ = x.shape
    tt = min(n, ROW_TILE)
    nb = ROW_TILE // tt
    return pl.pallas_call(
        functools.partial(_ffn_kernel, final_norm=final_norm),
        grid=(b // nb, n // tt),
        in_specs=[pl.BlockSpec((nb, tt, D_MODEL), lambda i, c: (i, c, 0)),
                  pl.BlockSpec((nb, HALO, FFN_DIM), lambda i, c: (i, 0, 0)),
                  _resident((1, D_MODEL)), w_up[1],
                  _resident((FFN_CONV, FFN_DIM)), _resident((1, FFN_DIM)),
                  w_down[1], _resident((1, D_MODEL))],
        out_specs=(pl.BlockSpec((nb, tt, D_MODEL), lambda i, c: (i, c, 0)),
                   pl.BlockSpec((nb, HALO, FFN_DIM), lambda i, c: (i, 0, 0))),
        out_shape=(jax.ShapeDtypeStruct((b, n, D_MODEL), F32),
                   jax.ShapeDtypeStruct((b, HALO, FFN_DIM), F32)),
        scratch_shapes=[pltpu.VMEM((nb, tt + HALO, FFN_DIM), F32),
                        pltpu.VMEM((nb * tt, FFN_DIM), BF16)],
        compiler_params=_cparams("parallel", "arbitrary"),
        name="ffn",
    )(x, cache, norm_w, w_up[0], conv_w, conv_b, w_down[0], final_w)


def _pad_cache(cache):
    return jnp.pad(cache, ((0, 0), (HALO - cache.shape[1], 0), (0, 0)))


def _split_w_in(w_in):
    sizes = (S5_WIDTH, SSD_INNER, SSD_CONV_DIM, SSD_HEADS, HG_WIDTH, HG_WIDTH, HG_WIDTH, HG_WIDTH, 3 * D_MODEL)
    offs = [0]
    for s in sizes:
        offs.append(offs[-1] + s)
    u, z, xbc, dt, q, f, i, g, gates = (w_in[..., offs[k]:offs[k + 1]] for k in range(len(sizes)))
    dt = jnp.pad(dt, ((0, 0),) * (dt.ndim - 1) + ((0, LANES - SSD_HEADS),))
    return u.astype(BF16), jnp.concatenate([z, xbc, q, f, i, g, gates, dt], axis=-1).astype(BF16)


def _run_trunk(x, s5_re, s5_im, ssd_h, ssd_buf, hg_s, ffn_buf, layers, final_w):
    b, n, _ = x.shape
    outs = ([], [], [], [], [], [])
    for l, w in enumerate(layers):
        yb, yc, gates, hb, bufb, sc = _mix(x, w["norm_mix_w"], w["w_mix"], ssd_h[l].reshape(b, -1, SSD_STATE),
                                           _pad_cache(ssd_buf[l]), hg_s[l].reshape(b, -1, HG_EXPAND),
                                           w["ssd"], w["hg_lb"], w["hg_norm_w"])
        h, hr, hi = _s5_merge(x, yb, yc, gates, w["norm_mix_w"], w["w_u"], s5_re[l].reshape(b, -1),
                              s5_im[l].reshape(b, -1), w["s5"], w["w_branch_a"], w["w_branch_b"], w["w_branch_c"],
                              w["w_out"])
        x, fb = _ffn(h, _pad_cache(ffn_buf[l]), w["norm_ffn_w"], w["ffn_w_up"],
                     w["ffn_conv_w"], w["ffn_conv_b"], w["ffn_w_down"], final_w, l == len(layers) - 1)
        vals = (hr.reshape(b, S5_GROUPS, S5_STATE), hi.reshape(b, S5_GROUPS, S5_STATE),
                hb.reshape(b, SSD_HEADS, SSD_HEAD_DIM, SSD_STATE), bufb[:, HALO - (SSD_CONV - 1):],
                sc.reshape(b, HG_HEADS, HG_EXPAND, HG_EXPAND), fb[:, HALO - (FFN_CONV - 1):])
        for lst, val in zip(outs, vals):
            lst.append(val)
    return (x,) + tuple(jnp.stack(v) for v in outs)


def kernel(x_prompt, x_sample, state_s5_re, state_s5_im, state_ssd, cache_ssd_conv, state_hgrn, cache_ffn_conv,
           norm_mix_w, w_in, s5_lambda_re, s5_lambda_im, s5_log_dt, s5_b_re, s5_b_im, s5_c_re, s5_c_im,
           s5_d, s5_glu_w, s5_glu_b, ssd_conv_w, ssd_conv_b, ssd_dt_bias, ssd_a_log, ssd_d, ssd_norm_w,
           hg_lb_logits, hg_norm_w, w_branch_a, w_branch_b, w_branch_c, w_out,
           norm_ffn_w, ffn_w_up, ffn_conv_w, ffn_conv_b, ffn_w_down, norm_final_w):
    lb_cum = jnp.cumsum(jax.nn.softmax(hg_lb_logits.astype(F32), axis=0), axis=0)
    hg_lb = lb_cum - lb_cum[0]
    w_u, w_mix = _split_w_in(w_in)
    glu_w, wa, wb, wc, wo, w_up, w_down = (w.astype(BF16) for w in (
        s5_glu_w, w_branch_a, w_branch_b, w_branch_c, w_out, ffn_w_up, ffn_w_down))
    layers = []
    for l in range(DEPTH):
        layers.append(dict(
            norm_mix_w=norm_mix_w[l].reshape(1, -1), w_u=_layer(w_u, l), w_mix=_layer(w_mix, l),
            s5=_s5_params(s5_lambda_re[l], s5_lambda_im[l], s5_log_dt[l], s5_b_re[l], s5_b_im[l],
                          s5_c_re[l], s5_c_im[l], s5_d[l], _layer(glu_w, l), s5_glu_b[l]),
            ssd=_ssd_params(ssd_conv_w[l], ssd_conv_b[l], ssd_dt_bias[l], ssd_a_log[l], ssd_d[l], ssd_norm_w[l]),
            hg_lb=hg_lb[l].reshape(1, -1), hg_norm_w=hg_norm_w[l].reshape(1, -1),
            w_branch_a=_layer(wa, l), w_branch_b=_layer(wb, l), w_branch_c=_layer(wc, l), w_out=_layer(wo, l),
            norm_ffn_w=norm_ffn_w[l].reshape(1, -1), ffn_w_up=_layer(w_up, l),
            ffn_conv_w=ffn_conv_w[l], ffn_conv_b=ffn_conv_b[l].reshape(1, -1),
            ffn_w_down=_layer(w_down, l)))
    final_w = norm_final_w.reshape(1, -1)
    bp = x_prompt.shape[0]
    zeros = lambda *s: jnp.zeros((DEPTH, bp) + s, F32)
    p = _run_trunk(x_prompt, zeros(S5_GROUPS, S5_STATE), zeros(S5_GROUPS, S5_STATE),
                   zeros(SSD_HEADS, SSD_HEAD_DIM, SSD_STATE), zeros(SSD_CONV - 1, SSD_CONV_DIM),
                   zeros(HG_HEADS, HG_EXPAND, HG_EXPAND), zeros(FFN_CONV - 1, FFN_DIM), layers, final_w)
    s = _run_trunk(x_sample, state_s5_re, state_s5_im, state_ssd, cache_ssd_conv, state_hgrn, cache_ffn_conv,
                   layers, final_w)
    return (p[0], s[0]) + p[1:] + s[1:]
```

```python
import functools

import jax
import jax.numpy as jnp
from jax import lax
from jax.experimental import pallas as pl
from jax.experimental.pallas import tpu as pltpu

F32 = jnp.float32
BF16 = jnp.bfloat16
EPS = 1e-6

D_MODEL = 1024
DEPTH = 2
S5_WIDTH = 512
S5_GROUP = 16
S5_GROUPS = 32
S5_STATE = 64
S5_SLABS = 4
SSD_INNER = 1024
SSD_HEAD_DIM = 64
SSD_HEADS = 16
SSD_GROUPS = 4
SSD_HPG = 4
SSD_STATE = 128
SSD_CONV = 4
SSD_CONV_DIM = 2048
HG_WIDTH = 512
HG_EXPAND = 128
HG_HEADS = 4
HG_CHUNK = 64
HG_SUB = 8
LOG2E = 1.4426950408889634
MIX_CHUNK = 128
MIX_ROWS = 256
PROJ_COL = 256
PIECE_COST = PROJ_COL * 54 // 64
CONV_COST = PROJ_COL * 12 // 32
FFN_DIM = 2816
FFN_CONV = 3
FFN_COL = 256
LANES = 128
HALO = 8
ROW_TILE = 512
VMEM_LIMIT = 56 * 1024 * 1024


def _cparams(*sem):
    return pltpu.CompilerParams(dimension_semantics=sem, vmem_limit_bytes=VMEM_LIMIT)


def _resident(shape):
    n = len(shape)
    return pl.BlockSpec(shape, lambda *_: (0,) * n, pipeline_mode=pl.Buffered(1))


def _layer(stacked, l):
    n = stacked.ndim - 1
    spec = pl.BlockSpec((None,) + stacked.shape[1:], lambda *_: (l,) + (0,) * n, pipeline_mode=pl.Buffered(1))
    return stacked, spec


def _rms(x, w):
    return x * lax.rsqrt(jnp.mean(x * x, axis=-1, keepdims=True) + EPS) * w


def _silu(x):
    return x * jax.nn.sigmoid(x)


def _dot(a, b):
    return jnp.dot(a, b, preferred_element_type=F32)


def _dot_nt(a, b):
    return lax.dot_general(a, b, (((1,), (1,)), ((), ())), preferred_element_type=F32)


def _dot_tn(a, b):
    return lax.dot_general(a, b, (((0,), (0,)), ((), ())), preferred_element_type=F32)


def _cumsum_rows(tri, x):
    return jnp.dot(tri, x, preferred_element_type=F32, precision=lax.Precision.HIGHEST)


def _tri(t):
    r = lax.broadcasted_iota(jnp.int32, (t, t), 0)
    c = lax.broadcasted_iota(jnp.int32, (t, t), 1)
    return r >= c


def _gelu_tanh(x):
    c = 0.7978845608028654
    return 0.5 * x * (1.0 + jnp.tanh(c * (x + 0.044715 * (x * x * x))))


def _s5_kernel(x_ref, yb_ref, yc_ref, gl_ref, nw_ref, wu_ref, h0r_ref, h0i_ref, ar_ref, ai_ref, bbd_ref, cbd_ref,
               d_ref, gw_ref, gb_ref, perm_ref, wa_ref, wb_ref, wc_ref, wo_ref,
               o_ref, hr_ref, hi_ref, ubm, utm, xs0, xs1, ytm, st, macc):
    nb, tt, _ = x_ref.shape
    half = S5_STATE * S5_GROUPS // S5_SLABS
    xs_bufs = (xs0, xs1)
    m = nb * tt
    d = D_MODEL
    mq = d // S5_SLABS

    def gate(k, cols):
        return jax.nn.sigmoid(gl_ref[:, :, k * d + cols.start:k * d + cols.stop].reshape(m, mq).astype(F32))

    @pl.when(pl.program_id(0) == 0)
    def _():
        for j in range(S5_SLABS):
            st[2 * j] = h0r_ref[:, j * half:(j + 1) * half]
            st[2 * j + 1] = h0i_ref[:, j * half:(j + 1) * half]

    hn = _rms(x_ref[...].reshape(nb * tt, D_MODEL), nw_ref[...]).astype(BF16)
    ubm[...] = _dot(hn, wu_ref[...]).reshape(nb, tt, S5_WIDTH)
    for t in range(tt):
        utm[t] = ubm[:, t, :]

    def project(j):
        u_slab = utm[:, :, j * LANES:(j + 1) * LANES].reshape(tt * nb, LANES)
        xs_bufs[j % 2][...] = _dot(u_slab.astype(BF16), bbd_ref[j])

    project(0)
    for j in range(S5_SLABS):
        xs = xs_bufs[j % 2]
        if j + 1 < S5_SLABS:
            project(j + 1)
        cols = slice(j * mq, (j + 1) * mq)
        macc[:, cols] = (gate(1, cols) * _dot(yb_ref[...].reshape(m, SSD_INNER), wb_ref[:, cols])
                         + gate(2, cols) * _dot(yc_ref[...].reshape(m, HG_WIDTH), wc_ref[:, cols]))
        ar = jnp.broadcast_to(ar_ref[:, j * half:(j + 1) * half], (nb, half))
        ai = jnp.broadcast_to(ai_ref[:, j * half:(j + 1) * half], (nb, half))
        sr, si = st[2 * j], st[2 * j + 1]
        for t in range(tt):
            rows = slice(t * nb, (t + 1) * nb)
            sr, si = (ar * sr - ai * si + xs[rows, 0:half], ar * si + ai * sr + xs[rows, half:2 * half])
            xs[rows, 0:half] = sr
            xs[rows, half:2 * half] = si
        st[2 * j] = sr
        st[2 * j + 1] = si
        hr_ref[:, j * half:(j + 1) * half] = sr
        hi_ref[:, j * half:(j + 1) * half] = si
        ytm[:, j * LANES:(j + 1) * LANES] = _dot(xs[...].astype(BF16), cbd_ref[j])

    u2 = utm[...].reshape(tt * nb, S5_WIDTH)

    y = ytm[...] + d_ref[...] * u2
    a = _gelu_tanh(y)
    out = a * jax.nn.sigmoid(_dot(a.astype(BF16), gw_ref[...]) + gb_ref[...])
    ya = _dot(perm_ref[...], out.astype(BF16)).astype(BF16)
    for j in range(S5_SLABS):
        cols = slice(j * mq, (j + 1) * mq)
        macc[:, cols] += gate(0, cols) * _dot(ya, wa_ref[:, cols])
    h = x_ref[...].reshape(m, d) + _dot(macc[...].astype(BF16), wo_ref[...])
    o_ref[...] = h.reshape(nb, tt, d)


def _time_to_batch_major(tt, nb):
    dst = jnp.arange(tt * nb)
    src = (dst % tt) * nb + dst // tt
    return (src[:, None] == jnp.arange(tt * nb)[None, :]).astype(BF16)


def _s5_merge(x, yb, yc, gates, norm_w, w_u, h0r, h0i, prm, wa, wb, wc, wo):
    b, n, _ = x.shape
    tt = ROW_TILE // b
    nst = S5_GROUPS * S5_STATE
    half = nst // S5_SLABS
    tok = lambda w: pl.BlockSpec((b, tt, w), lambda t: (0, t, 0))
    return pl.pallas_call(
        _s5_kernel,
        grid=(n // tt,),
        in_specs=[tok(D_MODEL), tok(SSD_INNER), tok(HG_WIDTH), tok(3 * D_MODEL),
                  _resident((1, D_MODEL)), w_u[1],
                  _resident((b, nst)), _resident((b, nst)),
                  _resident((1, nst)), _resident((1, nst)),
                  _resident((S5_SLABS, LANES, 2 * half)), _resident((S5_SLABS, 2 * half, LANES)),
                  _resident((1, S5_WIDTH)), prm["glu_w"][1], _resident((1, S5_WIDTH)),
                  _resident((tt * b, tt * b)), wa[1], wb[1], wc[1], wo[1]],
        out_specs=(tok(D_MODEL),
                   pl.BlockSpec((b, nst), lambda t: (0, 0)),
                   pl.BlockSpec((b, nst), lambda t: (0, 0))),
        out_shape=(jax.ShapeDtypeStruct((b, n, D_MODEL), F32),
                   jax.ShapeDtypeStruct((b, nst), F32),
                   jax.ShapeDtypeStruct((b, nst), F32)),
        scratch_shapes=[pltpu.VMEM((b, tt, S5_WIDTH), F32),
                        pltpu.VMEM((tt, b, S5_WIDTH), F32),
                        pltpu.VMEM((tt * b, 2 * half), F32), pltpu.VMEM((tt * b, 2 * half), F32),
                        pltpu.VMEM((tt * b, S5_WIDTH), F32),
                        pltpu.VMEM((2 * S5_SLABS, b, half), F32),
                        pltpu.VMEM((tt * b, D_MODEL), F32)],
        compiler_params=_cparams("arbitrary"),
        name="s5_merge",
    )(x, yb, yc, gates, norm_w, w_u[0], h0r, h0i, prm["ar"], prm["ai"], prm["bbd"], prm["cbd"], prm["d"],
      prm["glu_w"][0], prm["glu_b"], _time_to_batch_major(tt, b), wa[0], wb[0], wc[0], wo[0])


def _s5_params(lam_re, lam_im, log_dt, b_re, b_im, c_re, c_im, d_skip, glu_w, glu_b):
    dt = jnp.exp(log_dt)[:, None]
    mag = jnp.exp(lam_re * dt)
    ar, ai = mag * jnp.cos(lam_im * dt), mag * jnp.sin(lam_im * dt)
    den = lam_re * lam_re + lam_im * lam_im
    nr, ni = ar - 1.0, ai
    cr = (nr * lam_re + ni * lam_im) / den
    ci = (ni * lam_re - nr * lam_im) / den
    bbr = cr[..., None] * b_re - ci[..., None] * b_im
    bbi = cr[..., None] * b_im + ci[..., None] * b_re
    gps = S5_GROUPS // S5_SLABS
    eye = jnp.eye(gps, dtype=F32)

    def pack_b(bb):
        bb = bb.reshape(S5_SLABS, gps, S5_STATE, S5_GROUP)
        return jnp.einsum("sgph,gk->sghkp", bb, eye).reshape(S5_SLABS, gps * S5_GROUP, gps * S5_STATE)

    def pack_c(cc):
        cc = cc.reshape(S5_SLABS, gps, S5_GROUP, S5_STATE)
        return jnp.einsum("sghp,gk->sgpkh", cc, eye).reshape(S5_SLABS, gps * S5_STATE, gps * S5_GROUP)

    bbd = jnp.concatenate([pack_b(bbr), pack_b(bbi)], axis=2).astype(BF16)
    cbd = jnp.concatenate([pack_c(c_re), -pack_c(c_im)], axis=1).astype(BF16)
    return dict(ar=ar.reshape(1, -1), ai=ai.reshape(1, -1), bbd=bbd, cbd=cbd,
                d=d_skip.reshape(1, -1), glu_w=glu_w, glu_b=glu_b.reshape(1, -1))


def _ssd_conv_stages(rows, cw_ref, cb_ref, buf_ref, cbuf, xact):
    for c0 in range(0, SSD_CONV_DIM, PROJ_COL):
        yield "xbc%d" % c0
        cs = slice(c0, c0 + PROJ_COL)
        acc = cb_ref[:, cs] + cw_ref[SSD_CONV - 1:SSD_CONV, cs] * cbuf[HALO:HALO + rows, cs]
        for j in range(SSD_CONV - 1):
            off = HALO - (SSD_CONV - 1) + j
            acc = acc + cw_ref[j:j + 1, cs] * cbuf[off:off + rows, cs]
        xact[:, cs] = _silu(acc)
        tail = cbuf[rows:rows + HALO, cs]
        cbuf[0:HALO, cs] = tail
        buf_ref[0, :, cs] = tail
        yield CONV_COST * rows // MIX_CHUNK


def _ssd_stages(rows, z_ref, dt_ref, dtb_ref, alog_ref, dsk_ref, nw_ref, yb_ref, xact_ref, ht, ybuf, xwbuf):
    t = rows.stop - rows.start
    gs = SSD_HPG * SSD_HEAD_DIM
    xact = xact_ref.at[rows]
    yield "dt"
    dt = jax.nn.softplus(dt_ref[rows, :] + dtb_ref[...])
    la = dt * (-LOG2E * jnp.exp(alog_ref[...]))
    mask = _tri(t)
    cum = _cumsum_rows(mask.astype(F32), la)
    cum_t = cum.T
    last = cum[t - 1:t, :]
    ecum = jnp.exp2(cum)
    wgt = jnp.exp2(last - cum) * dt
    elast = jnp.exp2(last)
    yield 150

    b0 = SSD_INNER
    c0 = SSD_INNER + SSD_GROUPS * SSD_STATE
    for g in range(SSD_GROUPS):
        bg = xact[:, b0 + g * SSD_STATE:b0 + (g + 1) * SSD_STATE].astype(BF16)
        cg = xact[:, c0 + g * SSD_STATE:c0 + (g + 1) * SSD_STATE].astype(BF16)
        scores = _dot_nt(cg, bg)
        hg = ht[:, g * gs:(g + 1) * gs]
        yint = _dot(cg, hg.astype(BF16))
        for r in range(SSD_HPG):
            h = g * SSD_HPG + r
            hs = slice(h * SSD_HEAD_DIM, (h + 1) * SSD_HEAD_DIM)
            xh = xact[:, hs]
            seg = cum[:, h:h + 1] - cum_t[h:h + 1, :]
            decay = jnp.exp2(jnp.where(mask, seg, -jnp.inf))
            m = (scores * decay).astype(BF16)
            y = _dot(m, (dt[:, h:h + 1] * xh).astype(BF16))
            y = y + yint[:, r * SSD_HEAD_DIM:(r + 1) * SSD_HEAD_DIM] * ecum[:, h:h + 1]
            ybuf[:, hs] = y
            xwbuf[:, hs] = (wgt[:, h:h + 1] * xh).astype(BF16)
            ht[:, hs] = hg[:, r * SSD_HEAD_DIM:(r + 1) * SSD_HEAD_DIM] * elast[:, h:h + 1]
        ht[:, g * gs:(g + 1) * gs] += _dot_tn(bg, xwbuf[:, g * gs:(g + 1) * gs])
        yield 350

    yield "z"
    y = ybuf[...] + dsk_ref[...] * xact[:, :SSD_INNER]
    yb_ref[0, rows, :] = _rms(y * _silu(z_ref[rows, :]), nw_ref[...]).astype(yb_ref.dtype)
    yield 300


def _pad_lanes(v):
    return jnp.pad(v, (0, LANES - v.shape[0])).reshape(1, LANES)


def _ssd_params(conv_w, conv_b, dt_bias, a_log, d_skip, norm_w):
    return dict(conv_w=conv_w, conv_b=conv_b.reshape(1, -1), dt_bias=_pad_lanes(dt_bias), a_log=_pad_lanes(a_log),
                d=jnp.repeat(d_skip, SSD_HEAD_DIM).reshape(1, -1), norm_w=norm_w.reshape(1, -1))


def _hgrn_stages(rows, q_ref, f_ref, i_ref, g_ref, lb, nw, yc_ref, st, att_s):
    t = rows.stop - rows.start
    nsub = t // HG_SUB
    k = HG_EXPAND
    yield "qf"
    q = q_ref[rows, :]
    fz = f_ref[rows, :]
    e = jnp.exp(-jnp.abs(fz))
    lsp = jnp.minimum(fz, 0.0) - jnp.log(1.0 + e)
    lsn = lsp - fz
    lbt = jnp.log(lb) + lsn
    log_f = jnp.maximum(lsp, lbt) + jnp.log(1.0 + jnp.exp(-jnp.abs(lsp - lbt)))
    kk = (1.0 - lb) * (jnp.where(fz >= 0.0, e, 1.0) / (1.0 + e))

    cum = _cumsum_rows(_tri(t).astype(F32), log_f * LOG2E)
    last = cum[t - 1:t, :]
    qe = (q * jnp.exp2(cum)).astype(BF16)
    ke = (kk * jnp.exp2(last - cum)).astype(BF16)
    elast = jnp.exp2(last)
    yield 350

    c3 = cum.reshape(nsub, HG_SUB, HG_WIDTH)
    q3 = q.reshape(nsub, HG_SUB, HG_WIDTH)
    rrep = jnp.broadcast_to(c3[:, HG_SUB - 1:HG_SUB, :], c3.shape)
    kl = (kk * jnp.exp2((rrep - c3).reshape(t, HG_WIDTH))).astype(BF16)
    pairs = [(i, j) for i in range(1, nsub) for j in range(i)]
    if pairs:
        lhs = jnp.concatenate([q3[i] * jnp.exp2(c3[i] - rrep[j]) for i, j in pairs], axis=0).astype(BF16)
    col = lax.broadcasted_iota(jnp.int32, (HG_SUB, t), 1)
    colblk = col // HG_SUB
    row1 = lax.broadcasted_iota(jnp.int32, (HG_SUB, 1), 0)
    diag = col - row1

    for h in range(HG_HEADS):
        hs = slice(h * k, (h + 1) * k)
        if pairs:
            blk = _dot_nt(lhs[:, hs], kl[:, hs])
        for i in range(nsub):
            acc = jnp.zeros((HG_SUB, t), F32)
            for p, (pi, pj) in enumerate(pairs):
                if pi == i:
                    acc = jnp.where(colblk == pj, blk[p * HG_SUB:(p + 1) * HG_SUB, :], acc)
            att_s[h, i * HG_SUB:(i + 1) * HG_SUB, :] = acc
    yield 150

    for i in range(nsub):
        sub = slice(i * HG_SUB, (i + 1) * HG_SUB)
        qi, ci, ki = q[sub], cum[sub], kk[sub]
        accs = [att_s[h, sub, :] for h in range(HG_HEADS)]
        for d in range(HG_SUB):
            if d:
                p = qi * jnp.exp2(jnp.minimum(ci - pltpu.roll(ci, d, 0), 0.0)) * pltpu.roll(ki, d, 0)
            else:
                p = qi * ki
            hit = jnp.where(row1 >= d, diag, -2 * t) == i * HG_SUB - d
            for h in range(HG_HEADS):
                a = jnp.sum(p[:, h * k:(h + 1) * k], axis=-1, keepdims=True)
                accs[h] = jnp.where(hit, a, accs[h])
        for h in range(HG_HEADS):
            att_s[h, sub, :] = accs[h]
        if i % 2 == 1:
            yield 80

    yield "ig"
    iv = i_ref[rows, :].astype(BF16)
    for h in range(HG_HEADS):
        hs = slice(h * k, (h + 1) * k)
        sh = st[h]
        o = _dot(att_s[h].astype(BF16), iv[:, hs]) + _dot_nt(qe[:, hs], sh.astype(BF16))
        st[h] = sh * elast[:, hs] + _dot_tn(iv[:, hs], ke[:, hs])
        yc_ref[0, rows, hs] = (_rms(o, nw) * _silu(g_ref[rows, hs])).astype(yc_ref.dtype)
        if h % 2 == 1:
            yield 150


_MIX_COLS = {}
_off = 0
for _name, _width in (("z", SSD_INNER), ("xbc", SSD_CONV_DIM), ("q", HG_WIDTH), ("f", HG_WIDTH), ("i", HG_WIDTH),
                      ("g", HG_WIDTH), ("gates", 3 * D_MODEL), ("dt", LANES)):
    _MIX_COLS[_name] = (_off, _width)
    _off += _width
_MIX_WIDTH = _off


def _mix_kernel(x_ref, nw_ref, w_ref, h0_ref, cache_ref, s0_ref,
                cw_ref, cb_ref, dtb_ref, alog_ref, dsk_ref, snw_ref, lb_ref, hnw_ref,
                yb_ref, yc_ref, gates_ref, hn_ref, buf_ref, sn_ref,
                cbuf, xact, ht, ybuf, xwbuf, st, att_s, pz, pdt, pq, pf, pi, pg, hnb):
    rows = x_ref.shape[1]
    c = pl.program_id(1)
    gs = SSD_HPG * SSD_HEAD_DIM
    k = HG_EXPAND

    @pl.when(c == 0)
    def _():
        cbuf[0:HALO, :] = cache_ref[0]
        for g in range(SSD_GROUPS):
            ht[:, g * gs:(g + 1) * gs] = h0_ref[0, g * gs:(g + 1) * gs, :].T
        for h in range(HG_HEADS):
            st[h] = s0_ref[0, h * k:(h + 1) * k, :].T

    hnb[...] = _rms(x_ref[0], nw_ref[...]).astype(BF16)

    def piece(name, lo, width, store):
        off, _ = _MIX_COLS[name]
        return lambda: store(_dot(hnb[...], w_ref[:, off + lo:off + lo + width]), lo, width)

    def into(ref, row0=0):
        def store(v, lo, width):
            ref[row0:row0 + rows, lo:lo + width] = v.astype(ref.dtype)
        return store

    def into_gates(v, lo, width):
        gates_ref[0, :, lo:lo + width] = v.astype(gates_ref.dtype)

    pieces = [("xbc%d" % lo, piece("xbc", lo, PROJ_COL, into(cbuf, HALO))) for lo in range(0, SSD_CONV_DIM, PROJ_COL)]
    pieces.insert(1, ("dt", piece("dt", 0, LANES, into(pdt))))
    for tag, group in (("qf", (("q", pq), ("f", pf))), ("ig", (("i", pi), ("g", pg)))):
        pieces += [(tag, piece(name, lo, PROJ_COL, into(ref))) for name, ref in group
                   for lo in range(0, HG_WIDTH, PROJ_COL)]
    pieces += [("z", piece("z", lo, PROJ_COL, into(pz))) for lo in range(0, SSD_INNER, PROJ_COL)]
    pieces += [("gates", piece("gates", lo, PROJ_COL, into_gates)) for lo in range(0, 3 * D_MODEL, PROJ_COL)]

    def issue(n=1):
        for _ in range(min(n, len(pieces))):
            pieces.pop(0)[1]()

    def chain(gens):
        for gen in gens:
            yield from gen

    ts = min(rows, MIX_CHUNK)
    th = min(rows, HG_CHUNK)
    piece_cost = PIECE_COST * rows // MIX_CHUNK
    credit = [0]

    def advance(gen):
        for item in gen:
            if isinstance(item, str):
                while any(tag == item for tag, _ in pieces):
                    issue()
                continue
            credit[0] += item
            while credit[0] >= piece_cost and pieces:
                issue()
                credit[0] -= piece_cost
            return True
        return False

    issue(3)
    conv = _ssd_conv_stages(rows, cw_ref, cb_ref, buf_ref, cbuf, xact)
    while advance(conv):
        pass
    live = [chain([_ssd_stages(slice(r0, r0 + ts), pz, pdt, dtb_ref, alog_ref, dsk_ref, snw_ref, yb_ref,
                               xact, ht, ybuf, xwbuf) for r0 in range(0, rows, ts)]),
            chain([_hgrn_stages(slice(r0, r0 + th), pq, pf, pi, pg, lb_ref[...], hnw_ref[...], yc_ref, st, att_s)
                   for r0 in range(0, rows, th)])]
    while live:
        live = [gen for gen in live if advance(gen)]
    issue(len(pieces))

    @pl.when(c == pl.num_programs(1) - 1)
    def _():
        for g in range(SSD_GROUPS):
            hn_ref[0, g * gs:(g + 1) * gs, :] = ht[:, g * gs:(g + 1) * gs].T
        for h in range(HG_HEADS):
            sn_ref[0, h * k:(h + 1) * k, :] = st[h].T


def _mix(x, norm_w, w_mix, ssd_h0, ssd_cache, hg_s0, ssd, hg_lb, hg_norm_w):
    b, n, _ = x.shape
    t = min(n, MIX_ROWS)
    ts = min(t, MIX_CHUNK)
    th = min(t, HG_CHUNK)
    srows = SSD_HEADS * SSD_HEAD_DIM
    hrows = HG_HEADS * HG_EXPAND
    tok = lambda w: pl.BlockSpec((1, t, w), lambda i, c: (i, c, 0))
    per_seq = lambda r, w: pl.BlockSpec((1, r, w), lambda i, c: (i, 0, 0))
    return pl.pallas_call(
        _mix_kernel,
        grid=(b, n // t),
        in_specs=[tok(D_MODEL), _resident((1, D_MODEL)), w_mix[1],
                  per_seq(srows, SSD_STATE), per_seq(HALO, SSD_CONV_DIM), per_seq(hrows, HG_EXPAND),
                  _resident((SSD_CONV, SSD_CONV_DIM)), _resident((1, SSD_CONV_DIM)),
                  _resident((1, LANES)), _resident((1, LANES)),
                  _resident((1, SSD_INNER)), _resident((1, SSD_INNER)),
                  _resident((1, HG_WIDTH)), _resident((1, HG_EXPAND))],
        out_specs=(tok(SSD_INNER), tok(HG_WIDTH), tok(3 * D_MODEL),
                   per_seq(srows, SSD_STATE), per_seq(HALO, SSD_CONV_DIM), per_seq(hrows, HG_EXPAND)),
        out_shape=(jax.ShapeDtypeStruct((b, n, SSD_INNER), BF16),
                   jax.ShapeDtypeStruct((b, n, HG_WIDTH), BF16),
                   jax.ShapeDtypeStruct((b, n, 3 * D_MODEL), BF16),
                   jax.ShapeDtypeStruct((b, srows, SSD_STATE), F32),
                   jax.ShapeDtypeStruct((b, HALO, SSD_CONV_DIM), F32),
                   jax.ShapeDtypeStruct((b, hrows, HG_EXPAND), F32)),
        scratch_shapes=[pltpu.VMEM((t + HALO, SSD_CONV_DIM), F32),
                        pltpu.VMEM((t, SSD_CONV_DIM), F32),
                        pltpu.VMEM((SSD_STATE, srows), F32),
                        pltpu.VMEM((ts, SSD_INNER), F32),
                        pltpu.VMEM((ts, SSD_INNER), BF16),
                        pltpu.VMEM((HG_HEADS, HG_EXPAND, HG_EXPAND), F32),
                        pltpu.VMEM((HG_HEADS, th, th), F32),
                        pltpu.VMEM((t, SSD_INNER), F32),
                        pltpu.VMEM((t, LANES), F32),
                        pltpu.VMEM((t, HG_WIDTH), F32), pltpu.VMEM((t, HG_WIDTH), F32),
                        pltpu.VMEM((t, HG_WIDTH), F32), pltpu.VMEM((t, HG_WIDTH), F32),
                        pltpu.VMEM((t, D_MODEL), BF16)],
        compiler_params=_cparams("parallel", "arbitrary"),
        name="mix",
    )(x, norm_w, w_mix[0], ssd_h0, ssd_cache, hg_s0, ssd["conv_w"], ssd["conv_b"], ssd["dt_bias"], ssd["a_log"],
      ssd["d"], ssd["norm_w"], hg_lb, hg_norm_w)


def _ffn_kernel(x_ref, cache_ref, nw_ref, wup_ref, cw_ref, cb_ref, wdn_ref, fw_ref,
                o_ref, buf_ref, abuf, gbuf, *, final_norm):
    nb, tt, _ = x_ref.shape

    @pl.when(pl.program_id(1) == 0)
    def _():
        abuf[:, 0:HALO, :] = cache_ref[...]

    x = x_ref[...].reshape(nb * tt, D_MODEL)
    hn = _rms(x, nw_ref[...]).astype(BF16)
    for c0 in range(0, FFN_DIM, FFN_COL):
        cs = slice(c0, c0 + FFN_COL)
        a = _dot(hn, wup_ref[:, cs])
        v = _dot(hn, wup_ref[:, FFN_DIM + c0:FFN_DIM + c0 + FFN_COL])
        abuf[:, HALO:HALO + tt, cs] = a.reshape(nb, tt, FFN_COL)
        conv = cb_ref[:, cs] + cw_ref[FFN_CONV - 1:FFN_CONV, cs] * a.reshape(nb, tt, FFN_COL)
        for j in range(FFN_CONV - 1):
            off = HALO - (FFN_CONV - 1) + j
            conv = conv + cw_ref[j:j + 1, cs] * abuf[:, off:off + tt, cs]
        gbuf[:, cs] = (_silu(conv).reshape(nb * tt, FFN_COL) * v).astype(BF16)
    tail = abuf[:, tt:tt + HALO, :]
    abuf[:, 0:HALO, :] = tail
    buf_ref[...] = tail
    y = x + _dot(gbuf[...], wdn_ref[...])
    if final_norm:
        y = _rms(y, fw_ref[...])
    o_ref[...] = y.reshape(nb, tt, D_MODEL)


def _ffn(x, cache, norm_w, w_up, conv_w, conv_b, w_down, final_w, final_norm):
    b, n, _ = x.shape
    tt = min(n, ROW_TILE)
    nb = ROW_TILE // tt
    return pl.pallas_call(
        functools.partial(_ffn_kernel, final_norm=final_norm),
        grid=(b // nb, n // tt),
        in_specs=[pl.BlockSpec((nb, tt, D_MODEL), lambda i, c: (i, c, 0)),
                  pl.BlockSpec((nb, HALO, FFN_DIM), lambda i, c: (i, 0, 0)),
                  _resident((1, D_MODEL)), w_up[1],
                  _resident((FFN_CONV, FFN_DIM)), _resident((1, FFN_DIM)),
                  w_down[1], _resident((1, D_MODEL))],
        out_specs=(pl.BlockSpec((nb, tt, D_MODEL), lambda i, c: (i, c, 0)),
                   pl.BlockSpec((nb, HALO, FFN_DIM), lambda i, c: (i, 0, 0))),
        out_shape=(jax.ShapeDtypeStruct((b, n, D_MODEL), F32),
                   jax.ShapeDtypeStruct((b, HALO, FFN_DIM), F32)),
        scratch_shapes=[pltpu.VMEM((nb, tt + HALO, FFN_DIM), F32),
                        pltpu.VMEM((nb * tt, FFN_DIM), BF16)],
        compiler_params=_cparams("parallel", "arbitrary"),
        name="ffn",
    )(x, cache, norm_w, w_up[0], conv_w, conv_b, w_down[0], final_w)


def _pad_cache(cache):
    return jnp.pad(cache, ((0, 0), (HALO - cache.shape[1], 0), (0, 0)))


def _split_w_in(w_in):
    sizes = (S5_WIDTH, SSD_INNER, SSD_CONV_DIM, SSD_HEADS, HG_WIDTH, HG_WIDTH, HG_WIDTH, HG_WIDTH, 3 * D_MODEL)
    offs = [0]
    for s in sizes:
        offs.append(offs[-1] + s)
    u, z, xbc, dt, q, f, i, g, gates = (w_in[..., offs[k]:offs[k + 1]] for k in range(len(sizes)))
    dt = jnp.pad(dt, ((0, 0),) * (dt.ndim - 1) + ((0, LANES - SSD_HEADS),))
    return u.astype(BF16), jnp.concatenate([z, xbc, q, f, i, g, gates, dt], axis=-1).astype(BF16)


def _run_trunk(x, s5_re, s5_im, ssd_h, ssd_buf, hg_s, ffn_buf, layers, final_w):
    b, n, _ = x.shape
    outs = ([], [], [], [], [], [])
    for l, w in enumerate(layers):
        yb, yc, gates, hb, bufb, sc = _mix(x, w["norm_mix_w"], w["w_mix"], ssd_h[l].reshape(b, -1, SSD_STATE),
                                           _pad_cache(ssd_buf[l]), hg_s[l].reshape(b, -1, HG_EXPAND),
                                           w["ssd"], w["hg_lb"], w["hg_norm_w"])
        h, hr, hi = _s5_merge(x, yb, yc, gates, w["norm_mix_w"], w["w_u"], s5_re[l].reshape(b, -1),
                              s5_im[l].reshape(b, -1), w["s5"], w["w_branch_a"], w["w_branch_b"], w["w_branch_c"],
                              w["w_out"])
        x, fb = _ffn(h, _pad_cache(ffn_buf[l]), w["norm_ffn_w"], w["ffn_w_up"],
                     w["ffn_conv_w"], w["ffn_conv_b"], w["ffn_w_down"], final_w, l == len(layers) - 1)
        vals = (hr.reshape(b, S5_GROUPS, S5_STATE), hi.reshape(b, S5_GROUPS, S5_STATE),
                hb.reshape(b, SSD_HEADS, SSD_HEAD_DIM, SSD_STATE), bufb[:, HALO - (SSD_CONV - 1):],
                sc.reshape(b, HG_HEADS, HG_EXPAND, HG_EXPAND), fb[:, HALO - (FFN_CONV - 1):])
        for lst, val in zip(outs, vals):
            lst.append(val)
    return (x,) + tuple(jnp.stack(v) for v in outs)


def kernel(x_prompt, x_sample, state_s5_re, state_s5_im, state_ssd, cache_ssd_conv, state_hgrn, cache_ffn_conv,
           norm_mix_w, w_in, s5_lambda_re, s5_lambda_im, s5_log_dt, s5_b_re, s5_b_im, s5_c_re, s5_c_im,
           s5_d, s5_glu_w, s5_glu_b, ssd_conv_w, ssd_conv_b, ssd_dt_bias, ssd_a_log, ssd_d, ssd_norm_w,
           hg_lb_logits, hg_norm_w, w_branch_a, w_branch_b, w_branch_c, w_out,
           norm_ffn_w, ffn_w_up, ffn_conv_w, ffn_conv_b, ffn_w_down, norm_final_w):
    lb_cum = jnp.cumsum(jax.nn.softmax(hg_lb_logits.astype(F32), axis=0), axis=0)
    hg_lb = lb_cum - lb_cum[0]
    w_u, w_mix = _split_w_in(w_in)
    glu_w, wa, wb, wc, wo, w_up, w_down = (w.astype(BF16) for w in (
        s5_glu_w, w_branch_a, w_branch_b, w_branch_c, w_out, ffn_w_up, ffn_w_down))
    layers = []
    for l in range(DEPTH):
        layers.append(dict(
            norm_mix_w=norm_mix_w[l].reshape(1, -1), w_u=_layer(w_u, l), w_mix=_layer(w_mix, l),
            s5=_s5_params(s5_lambda_re[l], s5_lambda_im[l], s5_log_dt[l], s5_b_re[l], s5_b_im[l],
                          s5_c_re[l], s5_c_im[l], s5_d[l], _layer(glu_w, l), s5_glu_b[l]),
            ssd=_ssd_params(ssd_conv_w[l], ssd_conv_b[l], ssd_dt_bias[l], ssd_a_log[l], ssd_d[l], ssd_norm_w[l]),
            hg_lb=hg_lb[l].reshape(1, -1), hg_norm_w=hg_norm_w[l].reshape(1, -1),
            w_branch_a=_layer(wa, l), w_branch_b=_layer(wb, l), w_branch_c=_layer(wc, l), w_out=_layer(wo, l),
            norm_ffn_w=norm_ffn_w[l].reshape(1, -1), ffn_w_up=_layer(w_up, l),
            ffn_conv_w=ffn_conv_w[l], ffn_conv_b=ffn_conv_b[l].reshape(1, -1),
            ffn_w_down=_layer(w_down, l)))
    final_w = norm_final_w.reshape(1, -1)
    bp = x_prompt.shape[0]
    zeros = lambda *s: jnp.zeros((DEPTH, bp) + s, F32)
    p = _run_trunk(x_prompt, zeros(S5_GROUPS, S5_STATE), zeros(S5_GROUPS, S5_STATE),
                   zeros(SSD_HEADS, SSD_HEAD_DIM, SSD_STATE), zeros(SSD_CONV - 1, SSD_CONV_DIM),
                   zeros(HG_HEADS, HG_EXPAND, HG_EXPAND), zeros(FFN_CONV - 1, FFN_DIM), layers, final_w)
    s = _run_trunk(x_sample, state_s5_re, state_s5_im, state_ssd, cache_ssd_conv, state_hgrn, cache_ffn_conv,
                   layers, final_w)
    return (p[0], s[0]) + p[1:] + s[1:]
```

```python
import functools

import jax
import jax.numpy as jnp
from jax import lax
from jax.experimental import pallas as pl
from jax.experimental.pallas import tpu as pltpu

F32 = jnp.float32
BF16 = jnp.bfloat16
EPS = 1e-6

D_MODEL = 1024
DEPTH = 2
S5_WIDTH = 512
S5_GROUP = 16
S5_GROUPS = 32
S5_STATE = 64
S5_SLABS = 4
SSD_INNER = 1024
SSD_HEAD_DIM = 64
SSD_HEADS = 16
SSD_GROUPS = 4
SSD_HPG = 4
SSD_STATE = 128
SSD_CONV = 4
SSD_CONV_DIM = 2048
HG_WIDTH = 512
HG_EXPAND = 128
HG_HEADS = 4
HG_CHUNK = 64
HG_SUB = 8
LOG2E = 1.4426950408889634
MIX_CHUNK = 128
MIX_ROWS = 256
PROJ_COL = 256
PIECE_COST = PROJ_COL * 54 // 64
CONV_COST = PROJ_COL * 8 // 32
FFN_DIM = 2816
FFN_CONV = 3
FFN_COL = 256
LANES = 128
HALO = 8
ROW_TILE = 512
VMEM_LIMIT = 56 * 1024 * 1024


def _cparams(*sem):
    return pltpu.CompilerParams(dimension_semantics=sem, vmem_limit_bytes=VMEM_LIMIT)


def _resident(shape):
    n = len(shape)
    return pl.BlockSpec(shape, lambda *_: (0,) * n, pipeline_mode=pl.Buffered(1))


def _layer(stacked, l):
    n = stacked.ndim - 1
    spec = pl.BlockSpec((None,) + stacked.shape[1:], lambda *_: (l,) + (0,) * n, pipeline_mode=pl.Buffered(1))
    return stacked, spec


def _rms(x, w):
    return x * lax.rsqrt(jnp.mean(x * x, axis=-1, keepdims=True) + EPS) * w


def _silu(x):
    return x * jax.nn.sigmoid(x)


def _dot(a, b):
    return jnp.dot(a, b, preferred_element_type=F32)


def _dot_nt(a, b):
    return lax.dot_general(a, b, (((1,), (1,)), ((), ())), preferred_element_type=F32)


def _dot_tn(a, b):
    return lax.dot_general(a, b, (((0,), (0,)), ((), ())), preferred_element_type=F32)


def _cumsum_rows(tri, x):
    return jnp.dot(tri, x, preferred_element_type=F32, precision=lax.Precision.HIGHEST)


def _tri(t):
    r = lax.broadcasted_iota(jnp.int32, (t, t), 0)
    c = lax.broadcasted_iota(jnp.int32, (t, t), 1)
    return r >= c


def _gelu_tanh(x):
    c = 0.7978845608028654
    return 0.5 * x * (1.0 + jnp.tanh(c * (x + 0.044715 * (x * x * x))))


def _s5_kernel(x_ref, yb_ref, yc_ref, gl_ref, nw_ref, wu_ref, h0r_ref, h0i_ref, ar_ref, ai_ref, bbd_ref, cbd_ref,
               d_ref, gw_ref, gb_ref, perm_ref, wa_ref, wb_ref, wc_ref, wo_ref,
               o_ref, hr_ref, hi_ref, ubm, utm, xs0, xs1, ytm, st, macc):
    nb, tt, _ = x_ref.shape
    half = S5_STATE * S5_GROUPS // S5_SLABS
    xs_bufs = (xs0, xs1)
    m = nb * tt
    d = D_MODEL
    mq = d // S5_SLABS

    def gate(k, cols):
        return jax.nn.sigmoid(gl_ref[:, :, k * d + cols.start:k * d + cols.stop].reshape(m, mq).astype(F32))

    @pl.when(pl.program_id(0) == 0)
    def _():
        for j in range(S5_SLABS):
            st[2 * j] = h0r_ref[:, j * half:(j + 1) * half]
            st[2 * j + 1] = h0i_ref[:, j * half:(j + 1) * half]

    hn = _rms(x_ref[...].reshape(nb * tt, D_MODEL), nw_ref[...]).astype(BF16)
    ubm[...] = _dot(hn, wu_ref[...]).reshape(nb, tt, S5_WIDTH)
    for t in range(tt):
        utm[t] = ubm[:, t, :]

    def project(j):
        u_slab = utm[:, :, j * LANES:(j + 1) * LANES].reshape(tt * nb, LANES)
        xs_bufs[j % 2][...] = _dot(u_slab.astype(BF16), bbd_ref[j])

    project(0)
    for j in range(S5_SLABS):
        xs = xs_bufs[j % 2]
        if j + 1 < S5_SLABS:
            project(j + 1)
        cols = slice(j * mq, (j + 1) * mq)
        macc[:, cols] = (gate(1, cols) * _dot(yb_ref[...].reshape(m, SSD_INNER), wb_ref[:, cols])
                         + gate(2, cols) * _dot(yc_ref[...].reshape(m, HG_WIDTH), wc_ref[:, cols]))
        ar = jnp.broadcast_to(ar_ref[:, j * half:(j + 1) * half], (nb, half))
        ai = jnp.broadcast_to(ai_ref[:, j * half:(j + 1) * half], (nb, half))
        sr, si = st[2 * j], st[2 * j + 1]
        for t in range(tt):
            rows = slice(t * nb, (t + 1) * nb)
            sr, si = (ar * sr - ai * si + xs[rows, 0:half], ar * si + ai * sr + xs[rows, half:2 * half])
            xs[rows, 0:half] = sr
            xs[rows, half:2 * half] = si
        st[2 * j] = sr
        st[2 * j + 1] = si
        hr_ref[:, j * half:(j + 1) * half] = sr
        hi_ref[:, j * half:(j + 1) * half] = si
        ytm[:, j * LANES:(j + 1) * LANES] = _dot(xs[...].astype(BF16), cbd_ref[j])

    u2 = utm[...].reshape(tt * nb, S5_WIDTH)

    y = ytm[...] + d_ref[...] * u2
    a = _gelu_tanh(y)
    out = a * jax.nn.sigmoid(_dot(a.astype(BF16), gw_ref[...]) + gb_ref[...])
    ya = _dot(perm_ref[...], out.astype(BF16)).astype(BF16)
    for j in range(S5_SLABS):
        cols = slice(j * mq, (j + 1) * mq)
        macc[:, cols] += gate(0, cols) * _dot(ya, wa_ref[:, cols])
    h = x_ref[...].reshape(m, d) + _dot(macc[...].astype(BF16), wo_ref[...])
    o_ref[...] = h.reshape(nb, tt, d)


def _time_to_batch_major(tt, nb):
    dst = jnp.arange(tt * nb)
    src = (dst % tt) * nb + dst // tt
    return (src[:, None] == jnp.arange(tt * nb)[None, :]).astype(BF16)


def _s5_merge(x, yb, yc, gates, norm_w, w_u, h0r, h0i, prm, wa, wb, wc, wo):
    b, n, _ = x.shape
    tt = ROW_TILE // b
    nst = S5_GROUPS * S5_STATE
    half = nst // S5_SLABS
    tok = lambda w: pl.BlockSpec((b, tt, w), lambda t: (0, t, 0))
    return pl.pallas_call(
        _s5_kernel,
        grid=(n // tt,),
        in_specs=[tok(D_MODEL), tok(SSD_INNER), tok(HG_WIDTH), tok(3 * D_MODEL),
                  _resident((1, D_MODEL)), w_u[1],
                  _resident((b, nst)), _resident((b, nst)),
                  _resident((1, nst)), _resident((1, nst)),
                  _resident((S5_SLABS, LANES, 2 * half)), _resident((S5_SLABS, 2 * half, LANES)),
                  _resident((1, S5_WIDTH)), prm["glu_w"][1], _resident((1, S5_WIDTH)),
                  _resident((tt * b, tt * b)), wa[1], wb[1], wc[1], wo[1]],
        out_specs=(tok(D_MODEL),
                   pl.BlockSpec((b, nst), lambda t: (0, 0)),
                   pl.BlockSpec((b, nst), lambda t: (0, 0))),
        out_shape=(jax.ShapeDtypeStruct((b, n, D_MODEL), F32),
                   jax.ShapeDtypeStruct((b, nst), F32),
                   jax.ShapeDtypeStruct((b, nst), F32)),
        scratch_shapes=[pltpu.VMEM((b, tt, S5_WIDTH), F32),
                        pltpu.VMEM((tt, b, S5_WIDTH), F32),
                        pltpu.VMEM((tt * b, 2 * half), F32), pltpu.VMEM((tt * b, 2 * half), F32),
                        pltpu.VMEM((tt * b, S5_WIDTH), F32),
                        pltpu.VMEM((2 * S5_SLABS, b, half), F32),
                        pltpu.VMEM((tt * b, D_MODEL), F32)],
        compiler_params=_cparams("arbitrary"),
        name="s5_merge",
    )(x, yb, yc, gates, norm_w, w_u[0], h0r, h0i, prm["ar"], prm["ai"], prm["bbd"], prm["cbd"], prm["d"],
      prm["glu_w"][0], prm["glu_b"], _time_to_batch_major(tt, b), wa[0], wb[0], wc[0], wo[0])


def _s5_params(lam_re, lam_im, log_dt, b_re, b_im, c_re, c_im, d_skip, glu_w, glu_b):
    dt = jnp.exp(log_dt)[:, None]
    mag = jnp.exp(lam_re * dt)
    ar, ai = mag * jnp.cos(lam_im * dt), mag * jnp.sin(lam_im * dt)
    den = lam_re * lam_re + lam_im * lam_im
    nr, ni = ar - 1.0, ai
    cr = (nr * lam_re + ni * lam_im) / den
    ci = (ni * lam_re - nr * lam_im) / den
    bbr = cr[..., None] * b_re - ci[..., None] * b_im
    bbi = cr[..., None] * b_im + ci[..., None] * b_re
    gps = S5_GROUPS // S5_SLABS
    eye = jnp.eye(gps, dtype=F32)

    def pack_b(bb):
        bb = bb.reshape(S5_SLABS, gps, S5_STATE, S5_GROUP)
        return jnp.einsum("sgph,gk->sghkp", bb, eye).reshape(S5_SLABS, gps * S5_GROUP, gps * S5_STATE)

    def pack_c(cc):
        cc = cc.reshape(S5_SLABS, gps, S5_GROUP, S5_STATE)
        return jnp.einsum("sghp,gk->sgpkh", cc, eye).reshape(S5_SLABS, gps * S5_STATE, gps * S5_GROUP)

    bbd = jnp.concatenate([pack_b(bbr), pack_b(bbi)], axis=2).astype(BF16)
    cbd = jnp.concatenate([pack_c(c_re), -pack_c(c_im)], axis=1).astype(BF16)
    return dict(ar=ar.reshape(1, -1), ai=ai.reshape(1, -1), bbd=bbd, cbd=cbd,
                d=d_skip.reshape(1, -1), glu_w=glu_w, glu_b=glu_b.reshape(1, -1))


def _ssd_conv_stages(rows, cw_ref, cb_ref, buf_ref, cbuf, xact):
    for c0 in range(0, SSD_CONV_DIM, PROJ_COL):
        yield "xbc%d" % c0
        cs = slice(c0, c0 + PROJ_COL)
        acc = cb_ref[:, cs] + cw_ref[SSD_CONV - 1:SSD_CONV, cs] * cbuf[HALO:HALO + rows, cs]
        for j in range(SSD_CONV - 1):
            off = HALO - (SSD_CONV - 1) + j
            acc = acc + cw_ref[j:j + 1, cs] * cbuf[off:off + rows, cs]
        xact[:, cs] = _silu(acc)
        tail = cbuf[rows:rows + HALO, cs]
        cbuf[0:HALO, cs] = tail
        buf_ref[0, :, cs] = tail
        yield CONV_COST * rows // MIX_CHUNK


def _ssd_stages(rows, z_ref, dt_ref, dtb_ref, alog_ref, dsk_ref, nw_ref, yb_ref, xact_ref, ht, ybuf, xwbuf):
    t = rows.stop - rows.start
    gs = SSD_HPG * SSD_HEAD_DIM
    xact = xact_ref.at[rows]
    yield "dt"
    dt = jax.nn.softplus(dt_ref[rows, :] + dtb_ref[...])
    la = dt * (-LOG2E * jnp.exp(alog_ref[...]))
    mask = _tri(t)
    cum = _cumsum_rows(mask.astype(F32), la)
    cum_t = cum.T
    last = cum[t - 1:t, :]
    ecum = jnp.exp2(cum)
    wgt = jnp.exp2(last - cum) * dt
    elast = jnp.exp2(last)
    yield 150

    b0 = SSD_INNER
    c0 = SSD_INNER + SSD_GROUPS * SSD_STATE
    for g in range(SSD_GROUPS):
        bg = xact[:, b0 + g * SSD_STATE:b0 + (g + 1) * SSD_STATE].astype(BF16)
        cg = xact[:, c0 + g * SSD_STATE:c0 + (g + 1) * SSD_STATE].astype(BF16)
        scores = _dot_nt(cg, bg)
        hg = ht[:, g * gs:(g + 1) * gs]
        yint = _dot(cg, hg.astype(BF16))
        for r in range(SSD_HPG):
            h = g * SSD_HPG + r
            hs = slice(h * SSD_HEAD_DIM, (h + 1) * SSD_HEAD_DIM)
            xh = xact[:, hs]
            seg = cum[:, h:h + 1] - cum_t[h:h + 1, :]
            decay = jnp.exp2(jnp.where(mask, seg, -jnp.inf))
            m = (scores * decay).astype(BF16)
            y = _dot(m, (dt[:, h:h + 1] * xh).astype(BF16))
            y = y + yint[:, r * SSD_HEAD_DIM:(r + 1) * SSD_HEAD_DIM] * ecum[:, h:h + 1]
            ybuf[:, hs] = y
            xwbuf[:, hs] = (wgt[:, h:h + 1] * xh).astype(BF16)
            ht[:, hs] = hg[:, r * SSD_HEAD_DIM:(r + 1) * SSD_HEAD_DIM] * elast[:, h:h + 1]
        ht[:, g * gs:(g + 1) * gs] += _dot_tn(bg, xwbuf[:, g * gs:(g + 1) * gs])
        yield 350

    yield "z"
    y = ybuf[...] + dsk_ref[...] * xact[:, :SSD_INNER]
    yb_ref[0, rows, :] = _rms(y * _silu(z_ref[rows, :]), nw_ref[...]).astype(yb_ref.dtype)
    yield 300


def _pad_lanes(v):
    return jnp.pad(v, (0, LANES - v.shape[0])).reshape(1, LANES)


def _ssd_params(conv_w, conv_b, dt_bias, a_log, d_skip, norm_w):
    return dict(conv_w=conv_w, conv_b=conv_b.reshape(1, -1), dt_bias=_pad_lanes(dt_bias), a_log=_pad_lanes(a_log),
                d=jnp.repeat(d_skip, SSD_HEAD_DIM).reshape(1, -1), norm_w=norm_w.reshape(1, -1))


def _hgrn_stages(rows, q_ref, f_ref, i_ref, g_ref, lb, nw, yc_ref, st, att_s):
    t = rows.stop - rows.start
    nsub = t // HG_SUB
    k = HG_EXPAND
    yield "qf"
    q = q_ref[rows, :]
    fz = f_ref[rows, :]
    e = jnp.exp(-jnp.abs(fz))
    lsp = jnp.minimum(fz, 0.0) - jnp.log(1.0 + e)
    lsn = lsp - fz
    lbt = jnp.log(lb) + lsn
    log_f = jnp.maximum(lsp, lbt) + jnp.log(1.0 + jnp.exp(-jnp.abs(lsp - lbt)))
    kk = (1.0 - lb) * (jnp.where(fz >= 0.0, e, 1.0) / (1.0 + e))

    cum = _cumsum_rows(_tri(t).astype(F32), log_f * LOG2E)
    last = cum[t - 1:t, :]
    qe = (q * jnp.exp2(cum)).astype(BF16)
    ke = (kk * jnp.exp2(last - cum)).astype(BF16)
    elast = jnp.exp2(last)
    yield 350

    c3 = cum.reshape(nsub, HG_SUB, HG_WIDTH)
    q3 = q.reshape(nsub, HG_SUB, HG_WIDTH)
    rrep = jnp.broadcast_to(c3[:, HG_SUB - 1:HG_SUB, :], c3.shape)
    kl = (kk * jnp.exp2((rrep - c3).reshape(t, HG_WIDTH))).astype(BF16)
    pairs = [(i, j) for i in range(1, nsub) for j in range(i)]
    if pairs:
        lhs = jnp.concatenate([q3[i] * jnp.exp2(c3[i] - rrep[j]) for i, j in pairs], axis=0).astype(BF16)
    col = lax.broadcasted_iota(jnp.int32, (HG_SUB, t), 1)
    colblk = col // HG_SUB
    row1 = lax.broadcasted_iota(jnp.int32, (HG_SUB, 1), 0)
    diag = col - row1

    for h in range(HG_HEADS):
        hs = slice(h * k, (h + 1) * k)
        if pairs:
            blk = _dot_nt(lhs[:, hs], kl[:, hs])
        for i in range(nsub):
            acc = jnp.zeros((HG_SUB, t), F32)
            for p, (pi, pj) in enumerate(pairs):
                if pi == i:
                    acc = jnp.where(colblk == pj, blk[p * HG_SUB:(p + 1) * HG_SUB, :], acc)
            att_s[h, i * HG_SUB:(i + 1) * HG_SUB, :] = acc
    yield 150

    for i in range(nsub):
        sub = slice(i * HG_SUB, (i + 1) * HG_SUB)
        qi, ci, ki = q[sub], cum[sub], kk[sub]
        accs = [att_s[h, sub, :] for h in range(HG_HEADS)]
        for d in range(HG_SUB):
            if d:
                p = qi * jnp.exp2(jnp.minimum(ci - pltpu.roll(ci, d, 0), 0.0)) * pltpu.roll(ki, d, 0)
            else:
                p = qi * ki
            hit = jnp.where(row1 >= d, diag, -2 * t) == i * HG_SUB - d
            for h in range(HG_HEADS):
                a = jnp.sum(p[:, h * k:(h + 1) * k], axis=-1, keepdims=True)
                accs[h] = jnp.where(hit, a, accs[h])
        for h in range(HG_HEADS):
            att_s[h, sub, :] = accs[h]
        if i % 2 == 1:
            yield 80

    yield "ig"
    iv = i_ref[rows, :].astype(BF16)
    for h in range(HG_HEADS):
        hs = slice(h * k, (h + 1) * k)
        sh = st[h]
        o = _dot(att_s[h].astype(BF16), iv[:, hs]) + _dot_nt(qe[:, hs], sh.astype(BF16))
        st[h] = sh * elast[:, hs] + _dot_tn(iv[:, hs], ke[:, hs])
        yc_ref[0, rows, hs] = (_rms(o, nw) * _silu(g_ref[rows, hs])).astype(yc_ref.dtype)
        if h % 2 == 1:
            yield 150


_MIX_COLS = {}
_off = 0
for _name, _width in (("z", SSD_INNER), ("xbc", SSD_CONV_DIM), ("q", HG_WIDTH), ("f", HG_WIDTH), ("i", HG_WIDTH),
                      ("g", HG_WIDTH), ("gates", 3 * D_MODEL), ("dt", LANES)):
    _MIX_COLS[_name] = (_off, _width)
    _off += _width
_MIX_WIDTH = _off


def _mix_kernel(x_ref, nw_ref, w_ref, h0_ref, cache_ref, s0_ref,
                cw_ref, cb_ref, dtb_ref, alog_ref, dsk_ref, snw_ref, lb_ref, hnw_ref,
                yb_ref, yc_ref, gates_ref, hn_ref, buf_ref, sn_ref,
                cbuf, xact, ht, ybuf, xwbuf, st, att_s, pz, pdt, pq, pf, pi, pg, hnb):
    rows = x_ref.shape[1]
    c = pl.program_id(1)
    gs = SSD_HPG * SSD_HEAD_DIM
    k = HG_EXPAND

    @pl.when(c == 0)
    def _():
        cbuf[0:HALO, :] = cache_ref[0]
        for g in range(SSD_GROUPS):
            ht[:, g * gs:(g + 1) * gs] = h0_ref[0, g * gs:(g + 1) * gs, :].T
        for h in range(HG_HEADS):
            st[h] = s0_ref[0, h * k:(h + 1) * k, :].T

    hnb[...] = _rms(x_ref[0], nw_ref[...]).astype(BF16)

    def piece(name, lo, width, store):
        off, _ = _MIX_COLS[name]
        return lambda: store(_dot(hnb[...], w_ref[:, off + lo:off + lo + width]), lo, width)

    def into(ref, row0=0):
        def store(v, lo, width):
            ref[row0:row0 + rows, lo:lo + width] = v.astype(ref.dtype)
        return store

    def into_gates(v, lo, width):
        gates_ref[0, :, lo:lo + width] = v.astype(gates_ref.dtype)

    pieces = [("xbc%d" % lo, piece("xbc", lo, PROJ_COL, into(cbuf, HALO))) for lo in range(0, SSD_CONV_DIM, PROJ_COL)]
    pieces.insert(1, ("dt", piece("dt", 0, LANES, into(pdt))))
    for tag, group in (("qf", (("q", pq), ("f", pf))), ("ig", (("i", pi), ("g", pg)))):
        pieces += [(tag, piece(name, lo, PROJ_COL, into(ref))) for name, ref in group
                   for lo in range(0, HG_WIDTH, PROJ_COL)]
    pieces += [("z", piece("z", lo, PROJ_COL, into(pz))) for lo in range(0, SSD_INNER, PROJ_COL)]
    pieces += [("gates", piece("gates", lo, PROJ_COL, into_gates)) for lo in range(0, 3 * D_MODEL, PROJ_COL)]

    def issue(n=1):
        for _ in range(min(n, len(pieces))):
            pieces.pop(0)[1]()

    def chain(gens):
        for gen in gens:
            yield from gen

    ts = min(rows, MIX_CHUNK)
    th = min(rows, HG_CHUNK)
    piece_cost = PIECE_COST * rows // MIX_CHUNK
    credit = [0]

    def advance(gen):
        for item in gen:
            if isinstance(item, str):
                while any(tag == item for tag, _ in pieces):
                    issue()
                continue
            credit[0] += item
            while credit[0] >= piece_cost and pieces:
                issue()
                credit[0] -= piece_cost
            return True
        return False

    issue(3)
    conv = _ssd_conv_stages(rows, cw_ref, cb_ref, buf_ref, cbuf, xact)
    while advance(conv):
        pass
    live = [chain([_ssd_stages(slice(r0, r0 + ts), pz, pdt, dtb_ref, alog_ref, dsk_ref, snw_ref, yb_ref,
                               xact, ht, ybuf, xwbuf) for r0 in range(0, rows, ts)]),
            chain([_hgrn_stages(slice(r0, r0 + th), pq, pf, pi, pg, lb_ref[...], hnw_ref[...], yc_ref, st, att_s)
                   for r0 in range(0, rows, th)])]
    while live:
        live = [gen for gen in live if advance(gen)]
    issue(len(pieces))

    @pl.when(c == pl.num_programs(1) - 1)
    def _():
        for g in range(SSD_GROUPS):
            hn_ref[0, g * gs:(g + 1) * gs, :] = ht[:, g * gs:(g + 1) * gs].T
        for h in range(HG_HEADS):
            sn_ref[0, h * k:(h + 1) * k, :] = st[h].T


def _mix(x, norm_w, w_mix, ssd_h0, ssd_cache, hg_s0, ssd, hg_lb, hg_norm_w):
    b, n, _ = x.shape
    t = min(n, MIX_ROWS)
    ts = min(t, MIX_CHUNK)
    th = min(t, HG_CHUNK)
    srows = SSD_HEADS * SSD_HEAD_DIM
    hrows = HG_HEADS * HG_EXPAND
    tok = lambda w: pl.BlockSpec((1, t, w), lambda i, c: (i, c, 0))
    per_seq = lambda r, w: pl.BlockSpec((1, r, w), lambda i, c: (i, 0, 0))
    return pl.pallas_call(
        _mix_kernel,
        grid=(b, n // t),
        in_specs=[tok(D_MODEL), _resident((1, D_MODEL)), w_mix[1],
                  per_seq(srows, SSD_STATE), per_seq(HALO, SSD_CONV_DIM), per_seq(hrows, HG_EXPAND),
                  _resident((SSD_CONV, SSD_CONV_DIM)), _resident((1, SSD_CONV_DIM)),
                  _resident((1, LANES)), _resident((1, LANES)),
                  _resident((1, SSD_INNER)), _resident((1, SSD_INNER)),
                  _resident((1, HG_WIDTH)), _resident((1, HG_EXPAND))],
        out_specs=(tok(SSD_INNER), tok(HG_WIDTH), tok(3 * D_MODEL),
                   per_seq(srows, SSD_STATE), per_seq(HALO, SSD_CONV_DIM), per_seq(hrows, HG_EXPAND)),
        out_shape=(jax.ShapeDtypeStruct((b, n, SSD_INNER), BF16),
                   jax.ShapeDtypeStruct((b, n, HG_WIDTH), BF16),
                   jax.ShapeDtypeStruct((b, n, 3 * D_MODEL), BF16),
                   jax.ShapeDtypeStruct((b, srows, SSD_STATE), F32),
                   jax.ShapeDtypeStruct((b, HALO, SSD_CONV_DIM), F32),
                   jax.ShapeDtypeStruct((b, hrows, HG_EXPAND), F32)),
        scratch_shapes=[pltpu.VMEM((t + HALO, SSD_CONV_DIM), F32),
                        pltpu.VMEM((t, SSD_CONV_DIM), F32),
                        pltpu.VMEM((SSD_STATE, srows), F32),
                        pltpu.VMEM((ts, SSD_INNER), F32),
                        pltpu.VMEM((ts, SSD_INNER), BF16),
                        pltpu.VMEM((HG_HEADS, HG_EXPAND, HG_EXPAND), F32),
                        pltpu.VMEM((HG_HEADS, th, th), F32),
                        pltpu.VMEM((t, SSD_INNER), F32),
                        pltpu.VMEM((t, LANES), F32),
                        pltpu.VMEM((t, HG_WIDTH), F32), pltpu.VMEM((t, HG_WIDTH), F32),
                        pltpu.VMEM((t, HG_WIDTH), F32), pltpu.VMEM((t, HG_WIDTH), F32),
                        pltpu.VMEM((t, D_MODEL), BF16)],
        compiler_params=_cparams("parallel", "arbitrary"),
        name="mix",
    )(x, norm_w, w_mix[0], ssd_h0, ssd_cache, hg_s0, ssd["conv_w"], ssd["conv_b"], ssd["dt_bias"], ssd["a_log"],
      ssd["d"], ssd["norm_w"], hg_lb, hg_norm_w)


def _ffn_kernel(x_ref, cache_ref, nw_ref, wup_ref, cw_ref, cb_ref, wdn_ref, fw_ref,
                o_ref, buf_ref, abuf, gbuf, *, final_norm):
    nb, tt, _ = x_ref.shape

    @pl.when(pl.program_id(1) == 0)
    def _():
        abuf[:, 0:HALO, :] = cache_ref[...]

    x = x_ref[...].reshape(nb * tt, D_MODEL)
    hn = _rms(x, nw_ref[...]).astype(BF16)
    for c0 in range(0, FFN_DIM, FFN_COL):
        cs = slice(c0, c0 + FFN_COL)
        a = _dot(hn, wup_ref[:, cs])
        v = _dot(hn, wup_ref[:, FFN_DIM + c0:FFN_DIM + c0 + FFN_COL])
        abuf[:, HALO:HALO + tt, cs] = a.reshape(nb, tt, FFN_COL)
        conv = cb_ref[:, cs] + cw_ref[FFN_CONV - 1:FFN_CONV, cs] * a.reshape(nb, tt, FFN_COL)
        for j in range(FFN_CONV - 1):
            off = HALO - (FFN_CONV - 1) + j
            conv = conv + cw_ref[j:j + 1, cs] * abuf[:, off:off + tt, cs]
        gbuf[:, cs] = (_silu(conv).reshape(nb * tt, FFN_COL) * v).astype(BF16)
    tail = abuf[:, tt:tt + HALO, :]
    abuf[:, 0:HALO, :] = tail
    buf_ref[...] = tail
    y = x + _dot(gbuf[...], wdn_ref[...])
    if final_norm:
        y = _rms(y, fw_ref[...])
    o_ref[...] = y.reshape(nb, tt, D_MODEL)


def _ffn(x, cache, norm_w, w_up, conv_w, conv_b, w_down, final_w, final_norm):
    b, n, _ = x.shape
    tt = min(n, ROW_TILE)
    nb = ROW_TILE // tt
    return pl.pallas_call(
        functools.partial(_ffn_kernel, final_norm=final_norm),
        grid=(b // nb, n // tt),
        in_specs=[pl.BlockSpec((nb, tt, D_MODEL), lambda i, c: (i, c, 0)),
                  pl.BlockSpec((nb, HALO, FFN_DIM), lambda i, c: (i, 0, 0)),
                  _resident((1, D_MODEL)), w_up[1],
                  _resident((FFN_CONV, FFN_DIM)), _resident((1, FFN_DIM)),
                  w_down[1], _resident((1, D_MODEL))],
        out_specs=(pl.BlockSpec((nb, tt, D_MODEL), lambda i, c: (i, c, 0)),
                   pl.BlockSpec((nb, HALO, FFN_DIM), lambda i, c: (i, 0, 0))),
        out_shape=(jax.ShapeDtypeStruct((b, n, D_MODEL), F32),
                   jax.ShapeDtypeStruct((b, HALO, FFN_DIM), F32)),
        scratch_shapes=[pltpu.VMEM((nb, tt + HALO, FFN_DIM), F32),
                        pltpu.VMEM((nb * tt, FFN_DIM), BF16)],
        compiler_params=_cparams("parallel", "arbitrary"),
        name="ffn",
    )(x, cache, norm_w, w_up[0], conv_w, conv_b, w_down[0], final_w)


def _pad_cache(cache):
    return jnp.pad(cache, ((0, 0), (HALO - cache.shape[1], 0), (0, 0)))


def _split_w_in(w_in):
    sizes = (S5_WIDTH, SSD_INNER, SSD_CONV_DIM, SSD_HEADS, HG_WIDTH, HG_WIDTH, HG_WIDTH, HG_WIDTH, 3 * D_MODEL)
    offs = [0]
    for s in sizes:
        offs.append(offs[-1] + s)
    u, z, xbc, dt, q, f, i, g, gates = (w_in[..., offs[k]:offs[k + 1]] for k in range(len(sizes)))
    dt = jnp.pad(dt, ((0, 0),) * (dt.ndim - 1) + ((0, LANES - SSD_HEADS),))
    return u.astype(BF16), jnp.concatenate([z, xbc, q, f, i, g, gates, dt], axis=-1).astype(BF16)


def _run_trunk(x, s5_re, s5_im, ssd_h, ssd_buf, hg_s, ffn_buf, layers, final_w):
    b, n, _ = x.shape
    outs = ([], [], [], [], [], [])
    for l, w in enumerate(layers):
        yb, yc, gates, hb, bufb, sc = _mix(x, w["norm_mix_w"], w["w_mix"], ssd_h[l].reshape(b, -1, SSD_STATE),
                                           _pad_cache(ssd_buf[l]), hg_s[l].reshape(b, -1, HG_EXPAND),
                                           w["ssd"], w["hg_lb"], w["hg_norm_w"])
        h, hr, hi = _s5_merge(x, yb, yc, gates, w["norm_mix_w"], w["w_u"], s5_re[l].reshape(b, -1),
                              s5_im[l].reshape(b, -1), w["s5"], w["w_branch_a"], w["w_branch_b"], w["w_branch_c"],
                              w["w_out"])
        x, fb = _ffn(h, _pad_cache(ffn_buf[l]), w["norm_ffn_w"], w["ffn_w_up"],
                     w["ffn_conv_w"], w["ffn_conv_b"], w["ffn_w_down"], final_w, l == len(layers) - 1)
        vals = (hr.reshape(b, S5_GROUPS, S5_STATE), hi.reshape(b, S5_GROUPS, S5_STATE),
                hb.reshape(b, SSD_HEADS, SSD_HEAD_DIM, SSD_STATE), bufb[:, HALO - (SSD_CONV - 1):],
                sc.reshape(b, HG_HEADS, HG_EXPAND, HG_EXPAND), fb[:, HALO - (FFN_CONV - 1):])
        for lst, val in zip(outs, vals):
            lst.append(val)
    return (x,) + tuple(jnp.stack(v) for v in outs)


def kernel(x_prompt, x_sample, state_s5_re, state_s5_im, state_ssd, cache_ssd_conv, state_hgrn, cache_ffn_conv,
           norm_mix_w, w_in, s5_lambda_re, s5_lambda_im, s5_log_dt, s5_b_re, s5_b_im, s5_c_re, s5_c_im,
           s5_d, s5_glu_w, s5_glu_b, ssd_conv_w, ssd_conv_b, ssd_dt_bias, ssd_a_log, ssd_d, ssd_norm_w,
           hg_lb_logits, hg_norm_w, w_branch_a, w_branch_b, w_branch_c, w_out,
           norm_ffn_w, ffn_w_up, ffn_conv_w, ffn_conv_b, ffn_w_down, norm_final_w):
    lb_cum = jnp.cumsum(jax.nn.softmax(hg_lb_logits.astype(F32), axis=0), axis=0)
    hg_lb = lb_cum - lb_cum[0]
    w_u, w_mix = _split_w_in(w_in)
    glu_w, wa, wb, wc, wo, w_up, w_down = (w.astype(BF16) for w in (
        s5_glu_w, w_branch_a, w_branch_b, w_branch_c, w_out, ffn_w_up, ffn_w_down))
    layers = []
    for l in range(DEPTH):
        layers.append(dict(
            norm_mix_w=norm_mix_w[l].reshape(1, -1), w_u=_layer(w_u, l), w_mix=_layer(w_mix, l),
            s5=_s5_params(s5_lambda_re[l], s5_lambda_im[l], s5_log_dt[l], s5_b_re[l], s5_b_im[l],
                          s5_c_re[l], s5_c_im[l], s5_d[l], _layer(glu_w, l), s5_glu_b[l]),
            ssd=_ssd_params(ssd_conv_w[l], ssd_conv_b[l], ssd_dt_bias[l], ssd_a_log[l], ssd_d[l], ssd_norm_w[l]),
            hg_lb=hg_lb[l].reshape(1, -1), hg_norm_w=hg_norm_w[l].reshape(1, -1),
            w_branch_a=_layer(wa, l), w_branch_b=_layer(wb, l), w_branch_c=_layer(wc, l), w_out=_layer(wo, l),
            norm_ffn_w=norm_ffn_w[l].reshape(1, -1), ffn_w_up=_layer(w_up, l),
            ffn_conv_w=ffn_conv_w[l], ffn_conv_b=ffn_conv_b[l].reshape(1, -1),
            ffn_w_down=_layer(w_down, l)))
    final_w = norm_final_w.reshape(1, -1)
    bp = x_prompt.shape[0]
    zeros = lambda *s: jnp.zeros((DEPTH, bp) + s, F32)
    p = _run_trunk(x_prompt, zeros(S5_GROUPS, S5_STATE), zeros(S5_GROUPS, S5_STATE),
                   zeros(SSD_HEADS, SSD_HEAD_DIM, SSD_STATE), zeros(SSD_CONV - 1, SSD_CONV_DIM),
                   zeros(HG_HEADS, HG_EXPAND, HG_EXPAND), zeros(FFN_CONV - 1, FFN_DIM), layers, final_w)
    s = _run_trunk(x_sample, state_s5_re, state_s5_im, state_ssd, cache_ssd_conv, state_hgrn, cache_ffn_conv,
                   layers, final_w)
    return (p[0], s[0]) + p[1:] + s[1:]
```
